```python
import math
import jax, jax.numpy as jnp
from jax import lax
import numpy as np

D_MODEL = 2048
BATCH = 16
SEQ = 2048
DEPTH = 4
DEC_BATCH = 32
DEC_SEQ = 16
PAST_LEN = 1024

CHUNK = 64
SSM_HEADS = 32
SSM_HEAD_DIM = 64
D_SSM = SSM_HEADS * SSM_HEAD_DIM
SSM_GROUPS = 8
SSM_STATE = 128
D_CONV = 4
CONV_DIM = D_SSM + 2 * SSM_GROUPS * SSM_STATE
SSD_BLOCK = 64
DT_MIN = 0.001
DT_MAX = 0.1
ATT_HEADS = 16
ATT_HEAD_DIM = 64
D_QK = ATT_HEADS * 2 * ATT_HEAD_DIM
D_ATT = ATT_HEADS * 2 * ATT_HEAD_DIM
ROPE_DIM = ATT_HEAD_DIM // 4
ROPE_THETA = 500000.0
Q_BLOCK = 128
D_FF = 5632
EPS = 1e-6
IDX_Z = D_SSM
IDX_XBC = IDX_Z + CONV_DIM
IDX_DT = IDX_XBC + SSM_HEADS
IDX_Q = IDX_DT + D_QK
IDX_K = IDX_Q + D_QK
IDX_V = IDX_K + D_ATT
IN_DIM = IDX_V + 2 * D_MODEL
SPLITS = [IDX_Z, IDX_XBC, IDX_DT, IDX_Q, IDX_K, IDX_V]

kernel_name = 'hybrid_ssd_diffattn_stream_step'


def rmsnorm(x, w):
    xf = x.astype(jnp.float32)
    y = xf * lax.rsqrt(jnp.mean(xf * xf, axis=-1, keepdims=True) + EPS)
    return (y * w.astype(jnp.float32)).astype(x.dtype)


def swiglu(x, w13, w2):
    a, b = jnp.split(x @ w13, 2, axis=-1)
    return (jax.nn.silu(a) * b) @ w2


def rope_tables(pos):
    inv = 1.0 / (ROPE_THETA ** (jnp.arange(0, ROPE_DIM, 2, dtype=jnp.float32) / ROPE_DIM))
    ang = pos.astype(jnp.float32)[:, None] * inv[None, :]
    return jnp.cos(ang), jnp.sin(ang)


def apply_partial_rope(x, cos, sin):
    c = cos[None, :, None, None, :]
    s = sin[None, :, None, None, :]
    x1 = x[..., :ROPE_DIM // 2].astype(jnp.float32)
    x2 = x[..., ROPE_DIM // 2:ROPE_DIM].astype(jnp.float32)
    rot = jnp.concatenate([x1 * c - x2 * s, x2 * c + x1 * s], axis=-1).astype(x.dtype)
    return jnp.concatenate([rot, x[..., ROPE_DIM:]], axis=-1)


def causal_conv(xbc, buf, w, b):
    T = xbc.shape[1]
    xp = jnp.concatenate([buf.astype(xbc.dtype), xbc], axis=1)
    y = b + sum(xp[:, j:j + T] * w[j] for j in range(D_CONV))
    return jax.nn.silu(y), xp[:, xp.shape[1] - (D_CONV - 1):]


def ssd_prompt(x, dt, a, bm, cm):
    B, T = x.shape[:2]
    nc = T // SSD_BLOCK
    R = SSM_HEADS // SSM_GROUPS
    L = SSD_BLOCK
    xb = x.reshape(B, nc, L, SSM_GROUPS, R, SSM_HEAD_DIM)
    dtb = dt.reshape(B, nc, L, SSM_GROUPS, R)
    bb = bm.reshape(B, nc, L, SSM_GROUPS, SSM_STATE)
    cb = cm.reshape(B, nc, L, SSM_GROUPS, SSM_STATE)
    acs = jnp.cumsum(dtb * a.reshape(SSM_GROUPS, R), axis=2)
    xdt = xb * dtb[..., None]
    seg = acs[:, :, :, None] - acs[:, :, None, :]
    causal = jnp.tril(jnp.ones((L, L), dtype=bool))[:, :, None, None]
    decay = jnp.exp(jnp.where(causal, seg, -jnp.inf))
    scores = jnp.einsum('bclgn,bcsgn->bclsg', cb, bb)
    y_diag = jnp.einsum('bclsg,bclsgr,bcsgrp->bclgrp', scores, decay, xdt)
    decay_end = jnp.exp(acs[:, :, -1:] - acs)
    blk_states = jnp.einsum('bclgn,bclgr,bclgrp->bcgrpn', bb, decay_end, xdt)
    blk_decay = jnp.exp(acs[:, :, -1])

    def step(h, inp):
        s, d = inp
        return h * d[..., None, None] + s, h

    h0 = jnp.zeros((B, SSM_GROUPS, R, SSM_HEAD_DIM, SSM_STATE), jnp.float32)
    h_fin, h_in = lax.scan(step, h0, (jnp.moveaxis(blk_states, 1, 0), jnp.moveaxis(blk_decay, 1, 0)))
    h_in = jnp.moveaxis(h_in, 0, 1)
    y_off = jnp.einsum('bclgn,bcgrpn,bclgr->bclgrp', cb, h_in, jnp.exp(acs))
    y = (y_diag + y_off).reshape(B, T, SSM_HEADS, SSM_HEAD_DIM)
    return y, h_fin.reshape(B, SSM_HEADS, SSM_HEAD_DIM, SSM_STATE)


def ssd_step(x, dt, a, bm, cm, h0):
    B, T = x.shape[:2]
    R = SSM_HEADS // SSM_GROUPS
    xg = jnp.moveaxis(x.reshape(B, T, SSM_GROUPS, R, SSM_HEAD_DIM), 1, 0)
    dtg = jnp.moveaxis(dt.reshape(B, T, SSM_GROUPS, R), 1, 0)
    ag = a.reshape(SSM_GROUPS, R)

    def step(h, inp):
        xt, dtt, bt, ct = inp
        h = h * jnp.exp(dtt * ag)[..., None, None] + jnp.einsum('bgr,bgrp,bgn->bgrpn', dtt, xt, bt)
        return h, jnp.einsum('bgrpn,bgn->bgrp', h, ct)

    h_fin, ys = lax.scan(step, h0.reshape(B, SSM_GROUPS, R, SSM_HEAD_DIM, SSM_STATE),
                         (xg, dtg, jnp.moveaxis(bm, 1, 0), jnp.moveaxis(cm, 1, 0)))
    y = jnp.moveaxis(ys, 0, 1).reshape(B, T, SSM_HEADS, SSM_HEAD_DIM)
    return y, h_fin.reshape(B, SSM_HEADS, SSM_HEAD_DIM, SSM_STATE)


def diff_core(q, k, v, qpos, kpos, lam):
    s = jnp.einsum('bqhcd,bkhcd->bhcqk', q, k).astype(jnp.float32) * (ATT_HEAD_DIM ** -0.5)
    vis = (kpos[None, :] // CHUNK) <= (qpos[:, None] // CHUNK)
    p = jax.nn.softmax(jnp.where(vis, s, -jnp.inf), axis=-1)
    w = p[:, :, 0] - lam * p[:, :, 1]
    return jnp.einsum('bhqk,bkhe->bqhe', w.astype(v.dtype), v)


def diff_attn_prompt(q, k, v, lam):
    B, T = q.shape[:2]
    nb = T // Q_BLOCK
    qb = jnp.moveaxis(q.reshape(B, nb, Q_BLOCK, ATT_HEADS, 2, ATT_HEAD_DIM), 1, 0)
    kpos = jnp.arange(T)
    starts = jnp.arange(nb) * Q_BLOCK

    def blk(args):
        qi, s0 = args
        return diff_core(qi, k, v, s0 + jnp.arange(Q_BLOCK), kpos, lam)

    o = lax.map(blk, (qb, starts))
    return jnp.moveaxis(o, 0, 1).reshape(B, T, ATT_HEADS, 2 * ATT_HEAD_DIM)


def mixer(u, pos, past_k, past_v, conv_buf, ssm_h, w_in, conv_w, conv_b, dt_bias, a_log, d_skip,
          ssm_norm, w_ssm_out, lam_q1, lam_k1, lam_q2, lam_k2, subln, w_att_out, b_gate, w_o, lam_init):
    B, T, _ = u.shape
    f32 = jnp.float32
    z, xbc, dt_raw, q, k, v, g = jnp.split(u @ w_in, SPLITS, axis=-1)
    xbc, new_conv = causal_conv(xbc, conv_buf, conv_w, conv_b)
    xs, bm, cm = jnp.split(xbc.astype(f32), [D_SSM, D_SSM + SSM_GROUPS * SSM_STATE], axis=-1)
    xs = xs.reshape(B, T, SSM_HEADS, SSM_HEAD_DIM)
    bm = bm.reshape(B, T, SSM_GROUPS, SSM_STATE)
    cm = cm.reshape(B, T, SSM_GROUPS, SSM_STATE)
    dt = jax.nn.softplus(dt_raw.astype(f32) + dt_bias.astype(f32))
    a = -jnp.exp(a_log.astype(f32))
    if ssm_h is None:
        y, new_h = ssd_prompt(xs, dt, a, bm, cm)
    else:
        y, new_h = ssd_step(xs, dt, a, bm, cm, ssm_h.astype(f32))
    y = (y + d_skip.astype(f32)[:, None] * xs).reshape(B, T, D_SSM) * jax.nn.silu(z.astype(f32))
    gs = D_SSM // SSM_GROUPS
    y = rmsnorm(y.reshape(B, T, SSM_GROUPS, gs), ssm_norm.reshape(SSM_GROUPS, gs))
    a_out = y.reshape(B, T, D_SSM).astype(u.dtype) @ w_ssm_out
    cos, sin = rope_tables(pos)
    q = apply_partial_rope(q.reshape(B, T, ATT_HEADS, 2, ATT_HEAD_DIM), cos, sin)
    k = apply_partial_rope(k.reshape(B, T, ATT_HEADS, 2, ATT_HEAD_DIM), cos, sin)
    v = v.reshape(B, T, ATT_HEADS, 2 * ATT_HEAD_DIM)
    lam = (jnp.exp(jnp.sum(lam_q1.astype(f32) * lam_k1.astype(f32)))
           - jnp.exp(jnp.sum(lam_q2.astype(f32) * lam_k2.astype(f32))) + lam_init)
    if past_k is None:
        o = diff_attn_prompt(q, k, v, lam)
    else:
        n_past = past_k.shape[1]
        keys = jnp.concatenate([past_k.astype(k.dtype), k], axis=1)
        vals = jnp.concatenate([past_v.astype(v.dtype), v], axis=1)
        o = diff_core(q, keys, vals, pos, jnp.arange(n_past + T), lam)
    o = rmsnorm(o, subln) * (1.0 - lam_init)
    b_out = o.reshape(B, T, D_ATT) @ w_att_out
    g_ssm, g_att = jnp.split(jax.nn.sigmoid(g + b_gate), 2, axis=-1)
    out = (g_ssm * a_out + g_att * b_out) @ w_o
    return out, new_conv, new_h.astype(u.dtype), k, v


def trunk(x, pos, cache_k, cache_v, state_conv, state_ssm, params):
    (norm_ffn1, w13_ffn1, w2_ffn1, norm_mix, w_in, conv_w, conv_b, dt_bias, a_log, d_skip,
     ssm_norm, w_ssm_out, lambda_q1, lambda_k1, lambda_q2, lambda_k2, subln, w_att_out,
     b_gate, w_o, norm_ffn2, w13_ffn2, w2_ffn2, norm_final) = params
    B = x.shape[0]
    ks, vs, convs, ssms = [], [], [], []
    for i in range(DEPTH):
        lam_init = 0.8 - 0.6 * math.exp(-0.3 * i)
        x = x + 0.5 * swiglu(rmsnorm(x, norm_ffn1[i]), w13_ffn1[i], w2_ffn1[i])
        if cache_k is None:
            past_k = past_v = ssm_h = None
            conv_buf = jnp.zeros((B, D_CONV - 1, CONV_DIM), x.dtype)
        else:
            past_k, past_v, conv_buf, ssm_h = cache_k[i], cache_v[i], state_conv[i], state_ssm[i]
        mix, c_new, h_new, k_new, v_new = mixer(
            rmsnorm(x, norm_mix[i]), pos, past_k, past_v, conv_buf, ssm_h, w_in[i], conv_w[i],
            conv_b[i], dt_bias[i], a_log[i], d_skip[i], ssm_norm[i], w_ssm_out[i], lambda_q1[i],
            lambda_k1[i], lambda_q2[i], lambda_k2[i], subln[i], w_att_out[i], b_gate[i], w_o[i], lam_init)
        x = x + mix
        x = x + 0.5 * swiglu(rmsnorm(x, norm_ffn2[i]), w13_ffn2[i], w2_ffn2[i])
        ks.append(k_new)
        vs.append(v_new)
        convs.append(c_new)
        ssms.append(h_new)
    return rmsnorm(x, norm_final), jnp.stack(ks), jnp.stack(vs), jnp.stack(convs), jnp.stack(ssms)


def setup_inputs(seed: int = 0) -> dict:
    key = jax.random.key(seed)
    ks = jax.random.split(key, 32)
    f32 = jnp.float32

    def nrm(k, shape, scale):
        return jax.random.normal(k, shape, f32) * scale

    dt0 = jnp.exp(jax.random.uniform(ks[10], (DEPTH, SSM_HEADS), f32, math.log(DT_MIN), math.log(DT_MAX)))
    return {
        'x_prompt': nrm(ks[0], (BATCH, SEQ, D_MODEL), 1.0),
        'x_sample': nrm(ks[1], (DEC_BATCH, DEC_SEQ, D_MODEL), 1.0),
        'cache_k': nrm(ks[2], (DEPTH, DEC_BATCH, PAST_LEN, ATT_HEADS, 2, ATT_HEAD_DIM), 1.0),
        'cache_v': nrm(ks[3], (DEPTH, DEC_BATCH, PAST_LEN, ATT_HEADS, 2 * ATT_HEAD_DIM), 1.0),
        'state_conv': nrm(ks[4], (DEPTH, DEC_BATCH, D_CONV - 1, CONV_DIM), 1.0),
        'state_ssm': nrm(ks[5], (DEPTH, DEC_BATCH, SSM_HEADS, SSM_HEAD_DIM, SSM_STATE), 0.1),
        'norm_ffn1': 1.0 + nrm(ks[6], (DEPTH, D_MODEL), 0.02),
        'w13_ffn1': nrm(ks[7], (DEPTH, D_MODEL, 2 * D_FF), D_MODEL ** -0.5),
        'w2_ffn1': nrm(ks[8], (DEPTH, D_FF, D_MODEL), D_FF ** -0.5),
        'norm_mix': 1.0 + nrm(ks[9], (DEPTH, D_MODEL), 0.02),
        'w_in': nrm(ks[11], (DEPTH, D_MODEL, IN_DIM), D_MODEL ** -0.5),
        'conv_w': nrm(ks[12], (DEPTH, D_CONV, CONV_DIM), D_CONV ** -0.5),
        'conv_b': nrm(ks[13], (DEPTH, CONV_DIM), 0.02),
        'dt_bias': dt0 + jnp.log(-jnp.expm1(-dt0)),
        'a_log': jnp.log(jax.random.uniform(ks[14], (DEPTH, SSM_HEADS), f32, 1.0, 16.0)),
        'd_skip': 1.0 + nrm(ks[15], (DEPTH, SSM_HEADS), 0.02),
        'ssm_norm': 1.0 + nrm(ks[16], (DEPTH, D_SSM), 0.02),
        'w_ssm_out': nrm(ks[17], (DEPTH, D_SSM, D_MODEL), D_SSM ** -0.5),
        'lambda_q1': nrm(ks[18], (DEPTH, ATT_HEAD_DIM), 0.1),
        'lambda_k1': nrm(ks[19], (DEPTH, ATT_HEAD_DIM), 0.1),
        'lambda_q2': nrm(ks[20], (DEPTH, ATT_HEAD_DIM), 0.1),
        'lambda_k2': nrm(ks[21], (DEPTH, ATT_HEAD_DIM), 0.1),
        'subln': 1.0 + nrm(ks[22], (DEPTH, 2 * ATT_HEAD_DIM), 0.02),
        'w_att_out': nrm(ks[23], (DEPTH, D_ATT, D_MODEL), D_ATT ** -0.5),
        'b_gate': nrm(ks[24], (DEPTH, 2 * D_MODEL), 0.02),
        'w_o': nrm(ks[25], (DEPTH, D_MODEL, D_MODEL), D_MODEL ** -0.5),
        'norm_ffn2': 1.0 + nrm(ks[26], (DEPTH, D_MODEL), 0.02),
        'w13_ffn2': nrm(ks[27], (DEPTH, D_MODEL, 2 * D_FF), D_MODEL ** -0.5),
        'w2_ffn2': nrm(ks[28], (DEPTH, D_FF, D_MODEL), D_FF ** -0.5),
        'norm_final': 1.0 + nrm(ks[29], (D_MODEL,), 0.02),
    }


def reference(x_prompt, x_sample, cache_k, cache_v, state_conv, state_ssm, norm_ffn1, w13_ffn1,
              w2_ffn1, norm_mix, w_in, conv_w, conv_b, dt_bias, a_log, d_skip, ssm_norm, w_ssm_out,
              lambda_q1, lambda_k1, lambda_q2, lambda_k2, subln, w_att_out, b_gate, w_o, norm_ffn2,
              w13_ffn2, w2_ffn2, norm_final):
    params = (norm_ffn1, w13_ffn1, w2_ffn1, norm_mix, w_in, conv_w, conv_b, dt_bias, a_log, d_skip,
              ssm_norm, w_ssm_out, lambda_q1, lambda_k1, lambda_q2, lambda_k2, subln, w_att_out,
              b_gate, w_o, norm_ffn2, w13_ffn2, w2_ffn2, norm_final)
    y_prompt, k_p, v_p, conv_p, ssm_p = trunk(
        x_prompt, jnp.arange(x_prompt.shape[1]), None, None, None, None, params)
    n_past = cache_k.shape[2]
    y_sample, k_s, v_s, conv_s, ssm_s = trunk(
        x_sample, n_past + jnp.arange(x_sample.shape[1]), cache_k, cache_v, state_conv, state_ssm, params)
    return (y_prompt, y_sample, k_p, v_p, conv_p, ssm_p, k_s, v_s, conv_s, ssm_s)
```

```python
import functools
import math

import jax
import jax.numpy as jnp
from jax import lax
from jax.experimental import pallas as pl
from jax.experimental.pallas import tpu as pltpu

F32 = jnp.float32
BF16 = jnp.bfloat16

CHUNK = 64
SSM_HEADS = 32
SSM_HEAD_DIM = 64
SSM_GROUPS = 8
SSM_STATE = 128
D_CONV = 4
ATT_HEADS = 16
ATT_HEAD_DIM = 64
ROPE_DIM = ATT_HEAD_DIM // 4
ROPE_THETA = 500000.0
EPS = 1e-6

LANES = 128
SUBLANES = 8
SSD_L = 128
ATT_TQ = 256
VMEM_LIMIT = 56 * 1024 * 1024
NEG = -1e30


def _cp(sem, vmem=VMEM_LIMIT):
    return pltpu.CompilerParams(dimension_semantics=sem, vmem_limit_bytes=vmem)


def _tile(dim, pref):
    t = min(dim, pref)
    while dim % t:
        t //= 2
    return t


def _sigmoid(x):
    return 1.0 / (1.0 + jnp.exp(-x))


def _nt(a, b):
    return lax.dot_general(a, b, (((1,), (1,)), ((), ())), preferred_element_type=F32)


def _tn(a, b):
    return lax.dot_general(a, b, (((0,), (0,)), ((), ())), preferred_element_type=F32)


def _rmsnorm_kernel(x_ref, g_ref, o_ref):
    x = x_ref[...]
    ms = jnp.mean(x * x, axis=-1, keepdims=True)
    o_ref[...] = (x * lax.rsqrt(ms + EPS) * g_ref[...]).astype(o_ref.dtype)


def _rmsnorm(x, g):
    m, d = x.shape
    tm = _tile(m, 512)
    return pl.pallas_call(
        _rmsnorm_kernel,
        grid=(m // tm,),
        in_specs=[pl.BlockSpec((tm, d), lambda i: (i, 0)), pl.BlockSpec((1, d), lambda i: (0, 0))],
        out_specs=pl.BlockSpec((tm, d), lambda i: (i, 0)),
        out_shape=jax.ShapeDtypeStruct((m, d), BF16),
        compiler_params=_cp(("parallel",)),
        name="rmsnorm",
    )(x, g.reshape(1, d))


def _swiglu_kernel(x_ref, w1_ref, w3_ref, o_ref):
    x = x_ref[...]
    a = jnp.dot(x, w1_ref[...], preferred_element_type=F32)
    b = jnp.dot(x, w3_ref[...], preferred_element_type=F32)
    o_ref[...] = (a * _sigmoid(a) * b).astype(o_ref.dtype)


def _swiglu_up(xn, w13):
    m, d = xn.shape
    f = w13.shape[1] // 2
    tm, tn = _tile(m, 1024), _tile(f, 512)
    nb = f // tn
    return pl.pallas_call(
        _swiglu_kernel,
        grid=(m // tm, nb),
        in_specs=[pl.BlockSpec((tm, d), lambda i, j: (i, 0)),
                  pl.BlockSpec((d, tn), lambda i, j: (0, j)),
                  pl.BlockSpec((d, tn), lambda i, j: (0, j + nb))],
        out_specs=pl.BlockSpec((tm, tn), lambda i, j: (i, j)),
        out_shape=jax.ShapeDtypeStruct((m, f), BF16),
        compiler_params=_cp(("parallel", "arbitrary")),
        name="swiglu_up",
    )(xn, w13, w13)


def _resid_norm_kernel(h_ref, w_ref, x_ref, g_ref, *o_refs, scale, final):
    acc = jnp.dot(h_ref[...], w_ref[...], preferred_element_type=F32)
    xnew = x_ref[...] + scale * acc
    ms = jnp.mean(xnew * xnew, axis=-1, keepdims=True)
    y = xnew * lax.rsqrt(ms + EPS) * g_ref[...]
    if final:
        o_refs[0][...] = y
    else:
        o_refs[0][...] = xnew
        o_refs[1][...] = y.astype(BF16)


def _resid_norm(h, w, x, g, scale, final=False):
    m, k = h.shape
    d = w.shape[1]
    tm = _tile(m, 256)
    row = pl.BlockSpec((tm, d), lambda i: (i, 0))
    if final:
        out_shape, out_specs = jax.ShapeDtypeStruct((m, d), F32), row
    else:
        out_shape = (jax.ShapeDtypeStruct((m, d), F32), jax.ShapeDtypeStruct((m, d), BF16))
        out_specs = (row, row)
    return pl.pallas_call(
        functools.partial(_resid_norm_kernel, scale=scale, final=final),
        grid=(m // tm,),
        in_specs=[pl.BlockSpec((tm, k), lambda i: (i, 0)),
                  pl.BlockSpec((k, d), lambda i: (0, 0), pipeline_mode=pl.Buffered(1)),
                  row,
                  pl.BlockSpec((1, d), lambda i: (0, 0))],
        out_specs=out_specs,
        out_shape=out_shape,
        compiler_params=_cp(("parallel",)),
        name="resid_norm",
    )(h, w, x, g.reshape(1, d))


def _proj_plain_kernel(x_ref, w_ref, *o_refs):
    acc = jnp.dot(x_ref[...], w_ref[...], preferred_element_type=F32)
    for o in o_refs:
        o[...] = acc.astype(o.dtype)


def _proj_dt_kernel(x_ref, w_ref, b_ref, o_ref):
    v = jnp.dot(x_ref[...], w_ref[...], preferred_element_type=F32) + b_ref[...]
    o_ref[...] = jnp.maximum(v, 0.0) + jnp.log1p(jnp.exp(-jnp.abs(v)))


def _proj_rope_kernel(x_ref, w_ref, c_ref, sp_ref, sm_ref, *o_refs, scale):
    acc = jnp.dot(x_ref[...], w_ref[...], preferred_element_type=F32)
    c, sp, sm = c_ref[...], sp_ref[...], sm_ref[...]
    half = ROPE_DIM // 2
    for cb in range(acc.shape[1] // LANES):
        sl = slice(cb * LANES, (cb + 1) * LANES)
        y = acc[:, sl]
        r = y * c + pltpu.roll(y, half, 1) * sp + pltpu.roll(y, LANES - half, 1) * sm
        if scale != 1.0:
            r = r * scale
        for o in o_refs:
            o[:, sl] = r.astype(o.dtype)


def _proj(xn, w, out_dtypes, kind="plain", extra=(), scale=1.0, period=None):
    m, d = xn.shape
    n = w.shape[1]
    tm, tn = _tile(m, 1024), _tile(n, 1024)
    if kind == "rope":
        tm = _tile(period, tm)
    in_specs = [pl.BlockSpec((tm, d), lambda i, j: (i, 0)), pl.BlockSpec((d, tn), lambda i, j: (0, j))]
    if kind == "plain":
        body = _proj_plain_kernel
    elif kind == "dt":
        body = _proj_dt_kernel
        in_specs.append(pl.BlockSpec((1, tn), lambda i, j: (0, j)))
    else:
        body = functools.partial(_proj_rope_kernel, scale=scale)
        nper = period // tm
        in_specs += [pl.BlockSpec((tm, LANES), lambda i, j: (i % nper, 0))] * 3
    out_shape = tuple(jax.ShapeDtypeStruct((m, n), dt) for dt in out_dtypes)
    out_specs = tuple(pl.BlockSpec((tm, tn), lambda i, j: (i, j)) for _ in out_dtypes)
    return pl.pallas_call(
        body,
        grid=(m // tm, n // tn),
        in_specs=in_specs,
        out_specs=out_specs,
        out_shape=out_shape,
        compiler_params=_cp(("parallel", "arbitrary")),
        name="proj_" + kind,
    )(xn, w, *extra)


def _ssd_kernel(*refs, has_state):
    if has_state:
        (xbc_ref, dt_ref, z_ref, cw_ref, cb_ref, alog_ref, dsk_ref, nw_ref, tail0_ref, h0_ref,
         y_ref, hout_ref, tail_s, h_s, act_s, y_s) = refs
    else:
        (xbc_ref, dt_ref, z_ref, cw_ref, cb_ref, alog_ref, dsk_ref, nw_ref,
         y_ref, hout_ref, tail_s, h_s, act_s, y_s) = refs
    L, N, P = SSD_L, SSM_STATE, SSM_HEAD_DIM
    ds = SSM_HEADS * P
    cdim = ds + 2 * SSM_GROUPS * N
    c = pl.program_id(1)

    @pl.when(c == 0)
    def _():
        if has_state:
            tail_s[...] = tail0_ref[0]
            h_s[...] = h0_ref[0]
        else:
            tail_s[...] = jnp.zeros_like(tail_s)
            h_s[...] = jnp.zeros_like(h_s)

    cw = 512
    row8 = lax.broadcasted_iota(jnp.int32, (SUBLANES, cw), 0)
    for cb in range(cdim // cw):
        sl = slice(cb * cw, (cb + 1) * cw)
        xc = xbc_ref[0, :, sl]
        tail8 = tail_s[:, sl]
        conv = cb_ref[:, sl] + xc * cw_ref[D_CONV - 1:D_CONV, sl]
        for k in range(1, D_CONV):
            rolled = pltpu.roll(xc, k, 0)
            fix = pltpu.roll(tail8, k, 0)
            first = jnp.where(row8 < k, fix, rolled[:SUBLANES])
            shifted = jnp.concatenate([first, rolled[SUBLANES:]], axis=0)
            conv = conv + shifted * cw_ref[D_CONV - 1 - k:D_CONV - k, sl]
        tail_s[:, sl] = xc[L - SUBLANES:, :]
        act_s[:, sl] = conv * _sigmoid(conv)

    dtv = dt_ref[0]
    da = dtv * (-jnp.exp(alog_ref[...]))
    ri = lax.broadcasted_iota(jnp.int32, (L, L), 0)
    ci = lax.broadcasted_iota(jnp.int32, (L, L), 1)
    tri = ri >= ci
    acs = jnp.dot(tri.astype(F32), da, precision=lax.Precision.HIGHEST, preferred_element_type=F32)
    acs_t = acs.T
    dt_t = dtv.T
    acs_end = acs[L - 1:L, :]
    lane = lax.broadcasted_iota(jnp.int32, (L, LANES), 1)
    lane1 = lax.broadcasted_iota(jnp.int32, (1, LANES), 1)
    heads_per_group = SSM_HEADS // SSM_GROUPS
    pairs_per_group = heads_per_group // 2

    for g in range(SSM_GROUPS):
        bg = act_s[:, ds + g * N: ds + (g + 1) * N]
        cg = act_s[:, ds + (SSM_GROUPS + g) * N: ds + (SSM_GROUPS + g + 1) * N]
        sc = _nt(cg.astype(BF16), bg.astype(BF16))
        for pr in range(pairs_per_group):
            pair = g * pairs_per_group + pr
            psl = slice(pair * LANES, (pair + 1) * LANES)
            xs_pair = act_s[:, psl]
            ht_pair = h_s[:, psl]
            ypair = jnp.zeros((L, LANES), F32)
            upd = jnp.zeros((N, LANES), F32)
            dec_lane = jnp.zeros((1, LANES), F32)
            for e in range(2):
                h = pair * 2 + e
                half = (lane >= e * P) & (lane < (e + 1) * P)
                colb = jnp.broadcast_to(acs[:, h:h + 1], (L, LANES))
                dtcolb = jnp.broadcast_to(dtv[:, h:h + 1], (L, LANES))
                rowb = acs_t[h:h + 1, :]
                dtrow = dt_t[h:h + 1, :]
                dec = jnp.exp(jnp.where(tri, colb - rowb, NEG))
                mh = (sc * dec * dtrow).astype(BF16)
                ce = (cg * jnp.exp(colb)).astype(BF16)
                xs_m = jnp.where(half, xs_pair, 0.0).astype(BF16)
                ht_m = jnp.where(half, ht_pair, 0.0).astype(BF16)
                lhs = jnp.concatenate([mh, ce], axis=1)
                rhs = jnp.concatenate([xs_m, ht_m], axis=0)
                ypair = ypair + jnp.dot(lhs, rhs, preferred_element_type=F32)
                endb = acs_end[:, h:h + 1]
                bw = (bg * (dtcolb * jnp.exp(endb - colb))).astype(BF16)
                upd = upd + _tn(bw, xs_m)
                half1 = (lane1 >= e * P) & (lane1 < (e + 1) * P)
                dec_lane = jnp.where(half1, jnp.exp(endb), dec_lane)
            h_s[:, psl] = ht_pair * dec_lane + upd
            y_s[:, psl] = ypair

    gs = ds // SSM_GROUPS
    for g in range(SSM_GROUPS):
        sl = slice(g * gs, (g + 1) * gs)
        z = z_ref[0, :, sl]
        y = (y_s[:, sl] + dsk_ref[:, sl] * act_s[:, sl]) * (z * _sigmoid(z))
        ms = jnp.mean(y * y, axis=-1, keepdims=True)
        y_ref[0, :, sl] = (y * lax.rsqrt(ms + EPS) * nw_ref[:, sl]).astype(y_ref.dtype)

    @pl.when(c == pl.num_programs(1) - 1)
    def _():
        hout_ref[0] = h_s[...]


def _ssd(xbc, dt, z, conv_w, conv_b, a_log, d_skip, ssm_norm, tail0=None, h0=None):
    bsz, t, cdim = xbc.shape
    ds = SSM_HEADS * SSM_HEAD_DIM
    L = SSD_L
    assert SSD_L == SSM_STATE == LANES and t % L == 0
    has_state = h0 is not None
    pad = LANES - SSM_HEADS
    alog = jnp.pad(a_log.astype(F32), (0, pad)).reshape(1, LANES)
    dsk = jnp.repeat(d_skip.astype(F32), SSM_HEAD_DIM).reshape(1, ds)
    const = lambda shape: pl.BlockSpec(shape, lambda b, c: (0,) * len(shape))
    in_specs = [pl.BlockSpec((1, L, cdim), lambda b, c: (b, c, 0)),
                pl.BlockSpec((1, L, LANES), lambda b, c: (b, c, 0)),
                pl.BlockSpec((1, L, ds), lambda b, c: (b, c, 0)),
                const((D_CONV, cdim)), const((1, cdim)), const((1, LANES)), const((1, ds)), const((1, ds))]
    args = [xbc, dt, z, conv_w, conv_b.reshape(1, cdim), alog, dsk, ssm_norm.reshape(1, ds)]
    if has_state:
        in_specs += [pl.BlockSpec((1, SUBLANES, cdim), lambda b, c: (b, 0, 0)),
                     pl.BlockSpec((1, SSM_STATE, ds), lambda b, c: (b, 0, 0))]
        args += [tail0, h0]
    return pl.pallas_call(
        functools.partial(_ssd_kernel, has_state=has_state),
        grid=(bsz, t // L),
        in_specs=in_specs,
        out_specs=(pl.BlockSpec((1, L, ds), lambda b, c: (b, c, 0)),
                   pl.BlockSpec((1, SSM_STATE, ds), lambda b, c: (b, 0, 0))),
        out_shape=(jax.ShapeDtypeStruct((bsz, t, ds), BF16),
                   jax.ShapeDtypeStruct((bsz, SSM_STATE, ds), F32)),
        scratch_shapes=[pltpu.VMEM((SUBLANES, cdim), F32), pltpu.VMEM((SSM_STATE, ds), F32),
                        pltpu.VMEM((L, cdim), F32), pltpu.VMEM((L, ds), F32)],
        compiler_params=_cp(("parallel", "arbitrary")),
        name="ssd",
    )(*args)


def _lam(lq1, lk1, lq2, lk2, lam_init):
    return (jnp.exp(jnp.sum(lq1[...] * lk1[...], axis=-1, keepdims=True))
            - jnp.exp(jnp.sum(lq2[...] * lk2[...], axis=-1, keepdims=True)) + lam_init)


def _split_q(q):
    lane = lax.broadcasted_iota(jnp.int32, q.shape, 1)
    zero = jnp.zeros_like(q)
    return jnp.where(lane < ATT_HEAD_DIM, q, zero), jnp.where(lane >= ATT_HEAD_DIM, q, zero)


def _subln(o, sub_ref, lam_init):
    ms = jnp.mean(o * o, axis=-1, keepdims=True)
    return o * lax.rsqrt(ms + EPS) * sub_ref[...] * (1.0 - lam_init)


def _attn_prompt_kernel(lq1, lk1, lq2, lk2, sub_ref, q_ref, k_ref, v_ref, o_ref, *, tq, lam_init):
    qi = pl.program_id(2)
    lam = _lam(lq1, lk1, lq2, lk2, lam_init)
    q1, q2 = _split_q(q_ref[0])
    start = pl.multiple_of(qi * tq, tq)
    kd = k_ref[0, pl.ds(start, tq), :]
    vd = v_ref[0, pl.ds(start, tq), :]
    ri = lax.broadcasted_iota(jnp.int32, (tq, tq), 0)
    ci = lax.broadcasted_iota(jnp.int32, (tq, tq), 1)
    vis = (ci // CHUNK) <= (ri // CHUNK)

    def first(qc):
        s = jnp.where(vis, _nt(qc, kd), NEG)
        m = jnp.max(s, axis=-1, keepdims=True)
        p = jnp.exp(s - m)
        return m, jnp.sum(p, axis=-1, keepdims=True), jnp.dot(p.astype(BF16), vd, preferred_element_type=F32)

    def step(j, carry):
        off = pl.multiple_of(j * tq, tq)
        kj = k_ref[0, pl.ds(off, tq), :]
        vj = v_ref[0, pl.ds(off, tq), :]
        out = []
        for qc, (m, l, a) in zip((q1, q2), (carry[:3], carry[3:])):
            s = _nt(qc, kj)
            mn = jnp.maximum(m, jnp.max(s, axis=-1, keepdims=True))
            alpha = jnp.exp(m - mn)
            p = jnp.exp(s - mn)
            out += [mn, alpha * l + jnp.sum(p, axis=-1, keepdims=True),
                    alpha * a + jnp.dot(p.astype(BF16), vj, preferred_element_type=F32)]
        return tuple(out)

    m1, l1, a1, m2, l2, a2 = lax.fori_loop(0, qi, step, first(q1) + first(q2))
    o = a1 / l1 - lam * (a2 / l2)
    o_ref[0] = _subln(o, sub_ref, lam_init).astype(o_ref.dtype)


def _attn_prompt(q, k, v, lams, subln, lam_init):
    bsz, t, dq = q.shape
    tq = _tile(t, ATT_TQ)
    assert tq % CHUNK == 0
    hw = 2 * ATT_HEAD_DIM
    vec = pl.BlockSpec((1, ATT_HEAD_DIM), lambda b, h, i: (0, 0))
    return pl.pallas_call(
        functools.partial(_attn_prompt_kernel, tq=tq, lam_init=lam_init),
        grid=(bsz, dq // hw, t // tq),
        in_specs=[vec, vec, vec, vec,
                  pl.BlockSpec((1, hw), lambda b, h, i: (0, 0)),
                  pl.BlockSpec((1, tq, hw), lambda b, h, i: (b, i, h)),
                  pl.BlockSpec((1, t, hw), lambda b, h, i: (b, 0, h)),
                  pl.BlockSpec((1, t, hw), lambda b, h, i: (b, 0, h))],
        out_specs=pl.BlockSpec((1, tq, hw), lambda b, h, i: (b, i, h)),
        out_shape=jax.ShapeDtypeStruct((bsz, t, dq), BF16),
        compiler_params=_cp(("parallel", "parallel", "arbitrary")),
        name="attn_prompt",
    )(*lams, subln.reshape(1, hw), q, k, v)


def _attn_decode_kernel(lq1, lk1, lq2, lk2, sub_ref, q_ref, kc_ref, vc_ref, kn_ref, vn_ref, o_ref, *, lam_init):
    lam = _lam(lq1, lk1, lq2, lk2, lam_init)
    hw = 2 * ATT_HEAD_DIM
    for h in range(q_ref.shape[2] // hw):
        sl = slice(h * hw, (h + 1) * hw)
        q1, q2 = _split_q(q_ref[0, :, sl])
        kc = kc_ref[0, 0, :, sl].astype(BF16)
        vc = vc_ref[0, 0, :, sl].astype(BF16)
        kn = kn_ref[0, :, sl]
        vn = vn_ref[0, :, sl]

        def probs(qc):
            s_c, s_n = _nt(qc, kc), _nt(qc, kn)
            m = jnp.maximum(jnp.max(s_c, axis=-1, keepdims=True), jnp.max(s_n, axis=-1, keepdims=True))
            p_c, p_n = jnp.exp(s_c - m), jnp.exp(s_n - m)
            l = jnp.sum(p_c, axis=-1, keepdims=True) + jnp.sum(p_n, axis=-1, keepdims=True)
            return p_c / l, p_n / l

        p1c, p1n = probs(q1)
        p2c, p2n = probs(q2)
        o = (jnp.dot((p1c - lam * p2c).astype(BF16), vc, preferred_element_type=F32)
             + jnp.dot((p1n - lam * p2n).astype(BF16), vn, preferred_element_type=F32))
        o_ref[0, :, sl] = _subln(o, sub_ref, lam_init).astype(o_ref.dtype)


def _attn_decode(q, kn, vn, cache_k, cache_v, layer, lams, subln, lam_init):
    bsz, t, dq = q.shape
    npast = cache_k.shape[2]
    hw = 2 * ATT_HEAD_DIM
    vec = pl.BlockSpec((1, ATT_HEAD_DIM), lambda b: (0, 0))
    tok = pl.BlockSpec((1, t, dq), lambda b: (b, 0, 0))
    cache = pl.BlockSpec((1, 1, npast, dq), lambda b: (layer, b, 0, 0))
    return pl.pallas_call(
        functools.partial(_attn_decode_kernel, lam_init=lam_init),
        grid=(bsz,),
        in_specs=[vec, vec, vec, vec, pl.BlockSpec((1, hw), lambda b: (0, 0)), tok, cache, cache, tok, tok],
        out_specs=tok,
        out_shape=jax.ShapeDtypeStruct((bsz, t, dq), BF16),
        compiler_params=_cp(("parallel",)),
        name="attn_decode",
    )(*lams, subln.reshape(1, hw), q, cache_k, cache_v, kn, vn)


def _merge_kernel(ys_ref, o_ref, wa_ref, wb_ref, g1_ref, g2_ref, b1_ref, b2_ref, out_ref):
    a = jnp.dot(ys_ref[...], wa_ref[...], preferred_element_type=F32)
    b = jnp.dot(o_ref[...], wb_ref[...], preferred_element_type=F32)
    g1 = _sigmoid(g1_ref[...] + b1_ref[...])
    g2 = _sigmoid(g2_ref[...] + b2_ref[...])
    out_ref[...] = (g1 * a + g2 * b).astype(out_ref.dtype)


def _merge(ys, o, w_ssm_out, w_att_out, g, b_gate):
    m, ka = ys.shape
    kb = o.shape[1]
    d = w_ssm_out.shape[1]
    tm, tn = _tile(m, 1024), _tile(d, 512)
    nb = d // tn
    bg = b_gate.reshape(1, 2 * d)
    return pl.pallas_call(
        _merge_kernel,
        grid=(m // tm, nb),
        in_specs=[pl.BlockSpec((tm, ka), lambda i, j: (i, 0)),
                  pl.BlockSpec((tm, kb), lambda i, j: (i, 0)),
                  pl.BlockSpec((ka, tn), lambda i, j: (0, j)),
                  pl.BlockSpec((kb, tn), lambda i, j: (0, j)),
                  pl.BlockSpec((tm, tn), lambda i, j: (i, j)),
                  pl.BlockSpec((tm, tn), lambda i, j: (i, j + nb)),
                  pl.BlockSpec((1, tn), lambda i, j: (0, j)),
                  pl.BlockSpec((1, tn), lambda i, j: (0, j + nb))],
        out_specs=pl.BlockSpec((tm, tn), lambda i, j: (i, j)),
        out_shape=jax.ShapeDtypeStruct((m, d), BF16),
        compiler_params=_cp(("parallel", "arbitrary")),
        name="merge",
    )(ys, o, w_ssm_out, w_att_out, g, g, bg, bg)


def _rope_lane_tables(pos):
    half = ROPE_DIM // 2
    inv = 1.0 / (ROPE_THETA ** (jnp.arange(0, ROPE_DIM, 2, dtype=F32) / ROPE_DIM))
    ang = pos.astype(F32)[:, None] * inv[None, :]
    cos, sin = jnp.cos(ang), jnp.sin(ang)
    d = jnp.arange(LANES) % ATT_HEAD_DIM
    lo, hi = d < half, (d >= half) & (d < ROPE_DIM)
    idx = jnp.where(hi, d - half, jnp.where(lo, d, 0))
    c = jnp.where((lo | hi)[None, :], cos[:, idx], 1.0)
    sp = jnp.where(hi[None, :], sin[:, idx], 0.0)
    sm = jnp.where(lo[None, :], -sin[:, idx], 0.0)
    return c, sp, sm


def _split_w_in(w_in, d_model):
    ds = SSM_HEADS * SSM_HEAD_DIM
    cdim = ds + 2 * SSM_GROUPS * SSM_STATE
    dqk = ATT_HEADS * 2 * ATT_HEAD_DIM
    edges = [0, ds, ds + cdim, ds + cdim + SSM_HEADS]
    edges += [edges[-1] + dqk, edges[-1] + 2 * dqk, edges[-1] + 3 * dqk, edges[-1] + 3 * dqk + 2 * d_model]
    parts = [w_in[:, :, a:b].astype(BF16) for a, b in zip(edges[:-1], edges[1:])]
    parts[2] = jnp.pad(parts[2], ((0, 0), (0, 0), (0, LANES - SSM_HEADS)))
    return parts


def _trunk(x, pos, cache_k, cache_v, state_conv, state_ssm, p):
    bsz, t, d = x.shape
    m = bsz * t
    depth = p["w13_ffn1"].shape[0]
    decode = cache_k is not None
    ds = SSM_HEADS * SSM_HEAD_DIM
    cdim = ds + 2 * SSM_GROUPS * SSM_STATE
    dqk = ATT_HEADS * 2 * ATT_HEAD_DIM
    tabs = _rope_lane_tables(pos)
    if decode:
        tabs = tuple(jnp.tile(tb, (bsz, 1)) for tb in tabs)
        period = m
    else:
        period = t
    t_pad = -(-t // SSD_L) * SSD_L
    dt_bias = jnp.pad(p["dt_bias"].astype(F32), ((0, 0), (0, LANES - SSM_HEADS)))

    x = x.reshape(m, d)
    xn = _rmsnorm(x, p["norm_ffn1"][0])
    ks, vs, convs, ssms = [], [], [], []
    y = None
    for i in range(depth):
        lam_init = 0.8 - 0.6 * math.exp(-0.3 * i)
        h = _swiglu_up(xn, p["w13_ffn1"][i])
        x, xn = _resid_norm(h, p["w2_ffn1"][i], x, p["norm_mix"][i], 0.5)
        (z,) = _proj(xn, p["w_z"][i], (F32,))
        (xbc,) = _proj(xn, p["w_xbc"][i], (F32,))
        (dt,) = _proj(xn, p["w_dt"][i], (F32,), kind="dt", extra=(dt_bias[i:i + 1],))
        (q,) = _proj(xn, p["w_q"][i], (BF16,), kind="rope", extra=tabs, scale=ATT_HEAD_DIM ** -0.5, period=period)
        k32, k16 = _proj(xn, p["w_k"][i], (F32, BF16), kind="rope", extra=tabs, period=period)
        v32, v16 = _proj(xn, p["w_v"][i], (F32, BF16))
        (g,) = _proj(xn, p["w_g"][i], (F32,))
        xbc3 = xbc.reshape(bsz, t, cdim)
        dt3 = dt.reshape(bsz, t, LANES)
        z3 = z.reshape(bsz, t, ds)
        if t_pad != t:
            padt = ((0, 0), (0, t_pad - t), (0, 0))
            xbc_in, dt_in, z_in = jnp.pad(xbc3, padt), jnp.pad(dt3, padt), jnp.pad(z3, padt)
        else:
            xbc_in, dt_in, z_in = xbc3, dt3, z3
        if decode:
            tail0 = jnp.pad(state_conv[i].astype(F32), ((0, 0), (SUBLANES - (D_CONV - 1), 0), (0, 0)))
            h0 = jnp.transpose(state_ssm[i].astype(F32).reshape(bsz, ds, SSM_STATE), (0, 2, 1))
            prev = state_conv[i].astype(F32)
        else:
            tail0 = h0 = None
            prev = jnp.zeros((bsz, D_CONV - 1, cdim), F32)
        ys, h_t = _ssd(xbc_in, dt_in, z_in, p["conv_w"][i], p["conv_b"][i], p["a_log"][i], p["d_skip"][i],
                       p["ssm_norm"][i], tail0, h0)
        ys = ys[:, :t].reshape(m, ds)
        convs.append(jnp.concatenate([prev, xbc3], axis=1)[:, t:])
        ssms.append(jnp.transpose(h_t, (0, 2, 1)).reshape(bsz, SSM_HEADS, SSM_HEAD_DIM, SSM_STATE))
        lams = tuple(p[n][i].reshape(1, ATT_HEAD_DIM).astype(F32)
                     for n in ("lambda_q1", "lambda_k1", "lambda_q2", "lambda_k2"))
        q3, k3, v3 = (a.reshape(bsz, t, dqk) for a in (q, k16, v16))
        if decode:
            o = _attn_decode(q3, k3, v3, cache_k, cache_v, i, lams, p["subln"][i], lam_init)
        else:
            o = _attn_prompt(q3, k3, v3, lams, p["subln"][i], lam_init)
        ks.append(k32.reshape(bsz, t, ATT_HEADS, 2, ATT_HEAD_DIM))
        vs.append(v32.reshape(bsz, t, ATT_HEADS, 2 * ATT_HEAD_DIM))
        mix = _merge(ys, o.reshape(m, dqk), p["w_ssm_out"][i], p["w_att_out"][i], g, p["b_gate"][i])
        x, xn = _resid_norm(mix, p["w_o"][i], x, p["norm_ffn2"][i], 1.0)
        h = _swiglu_up(xn, p["w13_ffn2"][i])
        if i + 1 < depth:
            x, xn = _resid_norm(h, p["w2_ffn2"][i], x, p["norm_ffn1"][i + 1], 0.5)
        else:
            y = _resid_norm(h, p["w2_ffn2"][i], x, p["norm_final"], 0.5, final=True)
    return y.reshape(bsz, t, d), jnp.stack(ks), jnp.stack(vs), jnp.stack(convs), jnp.stack(ssms)


def kernel(x_prompt, x_sample, cache_k, cache_v, state_conv, state_ssm, norm_ffn1, w13_ffn1, w2_ffn1, norm_mix, w_in, conv_w, conv_b, dt_bias, a_log, d_skip, ssm_norm, w_ssm_out, lambda_q1, lambda_k1, lambda_q2, lambda_k2, subln, w_att_out, b_gate, w_o, norm_ffn2, w13_ffn2, w2_ffn2, norm_final):
    d_model = x_prompt.shape[-1]
    w_z, w_xbc, w_dt, w_q, w_k, w_v, w_g = _split_w_in(w_in, d_model)
    p = dict(
        norm_ffn1=norm_ffn1, w13_ffn1=w13_ffn1.astype(BF16), w2_ffn1=w2_ffn1.astype(BF16), norm_mix=norm_mix,
        w_z=w_z, w_xbc=w_xbc, w_dt=w_dt, w_q=w_q, w_k=w_k, w_v=w_v, w_g=w_g,
        conv_w=conv_w, conv_b=conv_b, dt_bias=dt_bias, a_log=a_log, d_skip=d_skip, ssm_norm=ssm_norm,
        w_ssm_out=w_ssm_out.astype(BF16), lambda_q1=lambda_q1, lambda_k1=lambda_k1, lambda_q2=lambda_q2,
        lambda_k2=lambda_k2, subln=subln, w_att_out=w_att_out.astype(BF16), b_gate=b_gate,
        w_o=w_o.astype(BF16), norm_ffn2=norm_ffn2, w13_ffn2=w13_ffn2.astype(BF16), w2_ffn2=w2_ffn2.astype(BF16),
        norm_final=norm_final)
    depth, dec_b, n_past = cache_k.shape[:3]
    dqk = ATT_HEADS * 2 * ATT_HEAD_DIM
    y_p, k_p, v_p, conv_p, ssm_p = _trunk(x_prompt, jnp.arange(x_prompt.shape[1]), None, None, None, None, p)
    ck = cache_k.reshape(depth, dec_b, n_past, dqk)
    cv = cache_v.reshape(depth, dec_b, n_past, dqk)
    y_s, k_s, v_s, conv_s, ssm_s = _trunk(x_sample, n_past + jnp.arange(x_sample.shape[1]), ck, cv,
                                          state_conv, state_ssm, p)
    return (y_p, y_s, k_p, v_p, conv_p, ssm_p, k_s, v_s, conv_s, ssm_s)
```

```python
import functools
import math

import jax
import jax.numpy as jnp
from jax import lax
from jax.experimental import pallas as pl
from jax.experimental.pallas import tpu as pltpu

F32 = jnp.float32
BF16 = jnp.bfloat16

CHUNK = 64
SSM_HEADS = 32
SSM_HEAD_DIM = 64
SSM_GROUPS = 8
SSM_STATE = 128
D_CONV = 4
ATT_HEADS = 16
ATT_HEAD_DIM = 64
ROPE_DIM = ATT_HEAD_DIM // 4
ROPE_THETA = 500000.0
EPS = 1e-6

LANES = 128
SUBLANES = 8
SSD_L = 128
ATT_TQ = 256
ATT_HEADS_PER_STEP = 2
VMEM_LIMIT = 56 * 1024 * 1024
NEG = -1e30


def _cp(sem, vmem=VMEM_LIMIT):
    return pltpu.CompilerParams(dimension_semantics=sem, vmem_limit_bytes=vmem)


def _tile(dim, pref):
    t = min(dim, pref)
    while dim % t:
        t //= 2
    return t


def _sigmoid(x):
    return 1.0 / (1.0 + jnp.exp(-x))


def _nt(a, b):
    return lax.dot_general(a, b, (((1,), (1,)), ((), ())), preferred_element_type=F32)


def _tn(a, b):
    return lax.dot_general(a, b, (((0,), (0,)), ((), ())), preferred_element_type=F32)


def _rmsnorm_kernel(x_ref, g_ref, o_ref):
    x = x_ref[...]
    ms = jnp.mean(x * x, axis=-1, keepdims=True)
    o_ref[...] = (x * lax.rsqrt(ms + EPS) * g_ref[...]).astype(o_ref.dtype)


def _rmsnorm(x, g):
    m, d = x.shape
    tm = _tile(m, 512)
    return pl.pallas_call(
        _rmsnorm_kernel,
        grid=(m // tm,),
        in_specs=[pl.BlockSpec((tm, d), lambda i: (i, 0)), pl.BlockSpec((1, d), lambda i: (0, 0))],
        out_specs=pl.BlockSpec((tm, d), lambda i: (i, 0)),
        out_shape=jax.ShapeDtypeStruct((m, d), BF16),
        compiler_params=_cp(("parallel",)),
        name="rmsnorm",
    )(x, g.reshape(1, d))


def _swiglu_kernel(x_ref, w1_ref, w3_ref, o_ref):
    x = x_ref[...]
    a = jnp.dot(x, w1_ref[...], preferred_element_type=F32)
    b = jnp.dot(x, w3_ref[...], preferred_element_type=F32)
    o_ref[...] = (a * _sigmoid(a) * b).astype(o_ref.dtype)


def _swiglu_up(xn, w13):
    m, d = xn.shape
    f = w13.shape[1] // 2
    tm, tn = _tile(m, 1024), _tile(f, 512)
    nb = f // tn
    return pl.pallas_call(
        _swiglu_kernel,
        grid=(m // tm, nb),
        in_specs=[pl.BlockSpec((tm, d), lambda i, j: (i, 0)),
                  pl.BlockSpec((d, tn), lambda i, j: (0, j)),
                  pl.BlockSpec((d, tn), lambda i, j: (0, j + nb))],
        out_specs=pl.BlockSpec((tm, tn), lambda i, j: (i, j)),
        out_shape=jax.ShapeDtypeStruct((m, f), BF16),
        compiler_params=_cp(("parallel", "arbitrary")),
        name="swiglu_up",
    )(xn, w13, w13)


def _resid_norm_kernel(h_ref, w_ref, x_ref, g_ref, *o_refs, scale, final):
    acc = jnp.dot(h_ref[...], w_ref[...], preferred_element_type=F32)
    xnew = x_ref[...] + scale * acc
    ms = jnp.mean(xnew * xnew, axis=-1, keepdims=True)
    y = xnew * lax.rsqrt(ms + EPS) * g_ref[...]
    if final:
        o_refs[0][...] = y
    else:
        o_refs[0][...] = xnew
        o_refs[1][...] = y.astype(BF16)


def _resid_norm(h, w, x, g, scale, final=False):
    m, k = h.shape
    d = w.shape[1]
    tm = _tile(m, 256)
    row = pl.BlockSpec((tm, d), lambda i: (i, 0))
    if final:
        out_shape, out_specs = jax.ShapeDtypeStruct((m, d), F32), row
    else:
        out_shape = (jax.ShapeDtypeStruct((m, d), F32), jax.ShapeDtypeStruct((m, d), BF16))
        out_specs = (row, row)
    return pl.pallas_call(
        functools.partial(_resid_norm_kernel, scale=scale, final=final),
        grid=(m // tm,),
        in_specs=[pl.BlockSpec((tm, k), lambda i: (i, 0)),
                  pl.BlockSpec((k, d), lambda i: (0, 0), pipeline_mode=pl.Buffered(1)),
                  row,
                  pl.BlockSpec((1, d), lambda i: (0, 0))],
        out_specs=out_specs,
        out_shape=out_shape,
        compiler_params=_cp(("parallel",)),
        name="resid_norm",
    )(h, w, x, g.reshape(1, d))


def _proj_plain_kernel(x_ref, w_ref, *o_refs):
    acc = jnp.dot(x_ref[...], w_ref[...], preferred_element_type=F32)
    for o in o_refs:
        o[...] = acc.astype(o.dtype)


def _proj_dt_kernel(x_ref, w_ref, b_ref, o_ref):
    v = jnp.dot(x_ref[...], w_ref[...], preferred_element_type=F32) + b_ref[...]
    o_ref[...] = jnp.maximum(v, 0.0) + jnp.log1p(jnp.exp(-jnp.abs(v)))


def _proj_rope_kernel(x_ref, w_ref, c_ref, sp_ref, sm_ref, *o_refs, scale):
    acc = jnp.dot(x_ref[...], w_ref[...], preferred_element_type=F32)
    c, sp, sm = c_ref[...], sp_ref[...], sm_ref[...]
    half = ROPE_DIM // 2
    for cb in range(acc.shape[1] // LANES):
        sl = slice(cb * LANES, (cb + 1) * LANES)
        y = acc[:, sl]
        r = y * c + pltpu.roll(y, half, 1) * sp + pltpu.roll(y, LANES - half, 1) * sm
        if scale != 1.0:
            r = r * scale
        for o in o_refs:
            o[:, sl] = r.astype(o.dtype)


def _proj(xn, w, out_dtypes, kind="plain", extra=(), scale=1.0, period=None, stack=None):
    m, d = xn.shape
    n = w.shape[1]
    tm, tn = _tile(m, 1024), _tile(n, 1024)
    if kind == "rope":
        tm = _tile(period, tm)
    in_specs = [pl.BlockSpec((tm, d), lambda i, j: (i, 0)), pl.BlockSpec((d, tn), lambda i, j: (0, j))]
    if kind == "plain":
        body = _proj_plain_kernel
    elif kind == "dt":
        body = _proj_dt_kernel
        in_specs.append(pl.BlockSpec((1, tn), lambda i, j: (0, j)))
    else:
        body = functools.partial(_proj_rope_kernel, scale=scale)
        nper = period // tm
        in_specs += [pl.BlockSpec((tm, LANES), lambda i, j: (i % nper, 0))] * 3
    out_shape = [jax.ShapeDtypeStruct((m, n), dt) for dt in out_dtypes]
    out_specs = [pl.BlockSpec((tm, tn), lambda i, j: (i, j)) for _ in out_dtypes]
    args = [xn, w, *extra]
    aliases = {}
    if stack is not None:
        buf, layer, depth = stack
        out_shape[0] = jax.ShapeDtypeStruct((depth, m, n), out_dtypes[0])
        out_specs[0] = pl.BlockSpec((None, tm, tn), lambda i, j: (layer, i, j))
        if buf is not None:
            n_in = len(args)
            inner = body
            body = lambda *refs: inner(*refs[:n_in], *refs[n_in + 1:])
            in_specs.append(pl.BlockSpec(memory_space=pl.ANY))
            args.append(buf)
            aliases = {n_in: 0}
    return pl.pallas_call(
        body,
        grid=(m // tm, n // tn),
        in_specs=in_specs,
        out_specs=tuple(out_specs),
        out_shape=tuple(out_shape),
        input_output_aliases=aliases,
        compiler_params=_cp(("parallel", "arbitrary")),
        name="proj_" + kind,
    )(*args)


def _ssd_kernel(*refs, has_state):
    if has_state:
        (xbc_ref, dt_ref, z_ref, cw_ref, cb_ref, alog_ref, dsk_ref, nw_ref, tail0_ref, h0_ref,
         y_ref, hout_ref, tail_s, h_s, act_s, y_s) = refs
    else:
        (xbc_ref, dt_ref, z_ref, cw_ref, cb_ref, alog_ref, dsk_ref, nw_ref,
         y_ref, hout_ref, tail_s, h_s, act_s, y_s) = refs
    L, N, P = SSD_L, SSM_STATE, SSM_HEAD_DIM
    ds = SSM_HEADS * P
    cdim = ds + 2 * SSM_GROUPS * N
    c = pl.program_id(1)

    @pl.when(c == 0)
    def _():
        if has_state:
            tail_s[...] = tail0_ref[0]
            h_s[...] = h0_ref[0]
        else:
            tail_s[...] = jnp.zeros_like(tail_s)
            h_s[...] = jnp.zeros_like(h_s)

    cw = 512
    row8 = lax.broadcasted_iota(jnp.int32, (SUBLANES, cw), 0)
    for cb in range(cdim // cw):
        sl = slice(cb * cw, (cb + 1) * cw)
        x_top = xbc_ref[0, :SUBLANES, sl]
        tail8 = tail_s[:, sl]
        top = cb_ref[:, sl] + x_top * cw_ref[D_CONV - 1:D_CONV, sl]
        rest = cb_ref[:, sl] + xbc_ref[0, SUBLANES:, sl] * cw_ref[D_CONV - 1:D_CONV, sl]
        for k in range(1, D_CONV):
            wk = cw_ref[D_CONV - 1 - k:D_CONV - k, sl]
            first = jnp.where(row8 < k, pltpu.roll(tail8, k, 0), pltpu.roll(x_top, k, 0))
            top = top + first * wk
            rest = rest + xbc_ref[0, SUBLANES - k:L - k, sl] * wk
        tail_s[:, sl] = xbc_ref[0, L - SUBLANES:, sl]
        conv = jnp.concatenate([top, rest], axis=0)
        act_s[:, sl] = conv * _sigmoid(conv)

    dtv = dt_ref[0]
    da = dtv * (-jnp.exp(alog_ref[...]))
    ri = lax.broadcasted_iota(jnp.int32, (L, L), 0)
    ci = lax.broadcasted_iota(jnp.int32, (L, L), 1)
    tri = ri >= ci
    acs = jnp.dot(tri.astype(F32), da, precision=lax.Precision.HIGHEST, preferred_element_type=F32)
    acs_t = acs.T
    dt_t = dtv.T
    acs_end = acs[L - 1:L, :]
    lane = lax.broadcasted_iota(jnp.int32, (L, LANES), 1)
    lane1 = lax.broadcasted_iota(jnp.int32, (1, LANES), 1)
    heads_per_group = SSM_HEADS // SSM_GROUPS
    pairs_per_group = heads_per_group // 2

    for g in range(SSM_GROUPS):
        bg = act_s[:, ds + g * N: ds + (g + 1) * N]
        cg = act_s[:, ds + (SSM_GROUPS + g) * N: ds + (SSM_GROUPS + g + 1) * N]
        sc = _nt(cg.astype(BF16), bg.astype(BF16))
        for pr in range(pairs_per_group):
            pair = g * pairs_per_group + pr
            psl = slice(pair * LANES, (pair + 1) * LANES)
            xs_pair = act_s[:, psl]
            ht_pair = h_s[:, psl]
            ypair = jnp.zeros((L, LANES), F32)
            upd = jnp.zeros((N, LANES), F32)
            dec_lane = jnp.zeros((1, LANES), F32)
            for e in range(2):
                h = pair * 2 + e
                half = (lane >= e * P) & (lane < (e + 1) * P)
                colb = jnp.broadcast_to(acs[:, h:h + 1], (L, LANES))
                dtcolb = jnp.broadcast_to(dtv[:, h:h + 1], (L, LANES))
                rowb = acs_t[h:h + 1, :]
                dtrow = dt_t[h:h + 1, :]
                dec = jnp.exp(jnp.where(tri, colb - rowb, NEG))
                mh = (sc * dec * dtrow).astype(BF16)
                ce = (cg * jnp.exp(colb)).astype(BF16)
                xs_m = jnp.where(half, xs_pair, 0.0).astype(BF16)
                ht_m = jnp.where(half, ht_pair, 0.0).astype(BF16)
                lhs = jnp.concatenate([mh, ce], axis=1)
                rhs = jnp.concatenate([xs_m, ht_m], axis=0)
                ypair = ypair + jnp.dot(lhs, rhs, preferred_element_type=F32)
                endb = acs_end[:, h:h + 1]
                bw = (bg * (dtcolb * jnp.exp(endb - colb))).astype(BF16)
                upd = upd + _tn(bw, xs_m)
                half1 = (lane1 >= e * P) & (lane1 < (e + 1) * P)
                dec_lane = jnp.where(half1, jnp.exp(endb), dec_lane)
            h_s[:, psl] = ht_pair * dec_lane + upd
            y_s[:, psl] = ypair

    gs = ds // SSM_GROUPS
    for g in range(SSM_GROUPS):
        sl = slice(g * gs, (g + 1) * gs)
        z = z_ref[0, :, sl]
        y = (y_s[:, sl] + dsk_ref[:, sl] * act_s[:, sl]) * (z * _sigmoid(z))
        ms = jnp.mean(y * y, axis=-1, keepdims=True)
        y_ref[0, :, sl] = (y * lax.rsqrt(ms + EPS) * nw_ref[:, sl]).astype(y_ref.dtype)

    @pl.when(c == pl.num_programs(1) - 1)
    def _():
        hout_ref[0] = h_s[...]


def _ssd(xbc, dt, z, conv_w, conv_b, a_log, d_skip, ssm_norm, tail0=None, h0=None):
    bsz, t, cdim = xbc.shape
    ds = SSM_HEADS * SSM_HEAD_DIM
    L = SSD_L
    assert SSD_L == SSM_STATE == LANES and t % L == 0
    has_state = h0 is not None
    pad = LANES - SSM_HEADS
    alog = jnp.pad(a_log.astype(F32), (0, pad)).reshape(1, LANES)
    dsk = jnp.repeat(d_skip.astype(F32), SSM_HEAD_DIM).reshape(1, ds)
    const = lambda shape: pl.BlockSpec(shape, lambda b, c: (0,) * len(shape))
    in_specs = [pl.BlockSpec((1, L, cdim), lambda b, c: (b, c, 0)),
                pl.BlockSpec((1, L, LANES), lambda b, c: (b, c, 0)),
                pl.BlockSpec((1, L, ds), lambda b, c: (b, c, 0)),
                const((D_CONV, cdim)), const((1, cdim)), const((1, LANES)), const((1, ds)), const((1, ds))]
    args = [xbc, dt, z, conv_w, conv_b.reshape(1, cdim), alog, dsk, ssm_norm.reshape(1, ds)]
    if has_state:
        in_specs += [pl.BlockSpec((1, SUBLANES, cdim), lambda b, c: (b, 0, 0)),
                     pl.BlockSpec((1, SSM_STATE, ds), lambda b, c: (b, 0, 0))]
        args += [tail0, h0]
    return pl.pallas_call(
        functools.partial(_ssd_kernel, has_state=has_state),
        grid=(bsz, t // L),
        in_specs=in_specs,
        out_specs=(pl.BlockSpec((1, L, ds), lambda b, c: (b, c, 0)),
                   pl.BlockSpec((1, SSM_STATE, ds), lambda b, c: (b, 0, 0))),
        out_shape=(jax.ShapeDtypeStruct((bsz, t, ds), BF16),
                   jax.ShapeDtypeStruct((bsz, SSM_STATE, ds), F32)),
        scratch_shapes=[pltpu.VMEM((SUBLANES, cdim), F32), pltpu.VMEM((SSM_STATE, ds), F32),
                        pltpu.VMEM((L, cdim), F32), pltpu.VMEM((L, ds), F32)],
        compiler_params=_cp(("parallel", "arbitrary")),
        name="ssd",
    )(*args)


def _lam(lq1, lk1, lq2, lk2, lam_init):
    return (jnp.exp(jnp.sum(lq1[...] * lk1[...], axis=-1, keepdims=True))
            - jnp.exp(jnp.sum(lq2[...] * lk2[...], axis=-1, keepdims=True)) + lam_init)


def _split_q(q):
    lane = lax.broadcasted_iota(jnp.int32, q.shape, 1)
    zero = jnp.zeros_like(q)
    return jnp.where(lane < ATT_HEAD_DIM, q, zero), jnp.where(lane >= ATT_HEAD_DIM, q, zero)


def _subln(o, sub_ref, lam_init):
    ms = jnp.mean(o * o, axis=-1, keepdims=True)
    return o * lax.rsqrt(ms + EPS) * sub_ref[...] * (1.0 - lam_init)


def _attn_prompt_kernel(lq1, lk1, lq2, lk2, sub_ref, q_ref, k_ref, v_ref, o_ref,
                        vt_s, sa_s, sb_s, m_s, l_s, acc_s, *, tq, heads, lam_init):
    qi = pl.program_id(2)
    hw = 2 * ATT_HEAD_DIM
    t = k_ref.shape[1]

    @pl.when(qi == 0)
    def _():
        for g in range(heads):
            for cb in range(t // tq):
                rows = slice(cb * tq, (cb + 1) * tq)
                vt_s[g, :, rows] = v_ref[0, rows, g * hw:(g + 1) * hw].astype(F32).T.astype(BF16)

    start = pl.multiple_of(qi * tq, tq)
    kr = lax.broadcasted_iota(jnp.int32, (tq, 2 * tq), 0)
    qc = lax.broadcasted_iota(jnp.int32, (tq, 2 * tq), 1)
    qc = jnp.where(qc >= tq, qc - tq, qc)
    vis = (kr // CHUNK) <= (qc // CHUNK)
    qq = []
    for g in range(heads):
        q1, q2 = _split_q(q_ref[0, :, g * hw:(g + 1) * hw])
        qq.append(jnp.concatenate([q1, q2], axis=0))

    def scores(off, s_ref):
        for g in range(heads):
            s_ref[g] = _nt(k_ref[0, pl.ds(off, tq), g * hw:(g + 1) * hw], qq[g])

    def consume(off, s_ref, masked):
        for g in range(heads):
            s = s_ref[g]
            if masked:
                s = jnp.where(vis, s, NEG)
            m = m_s[g]
            mn = jnp.maximum(m, jnp.max(s, axis=0, keepdims=True))
            alpha = jnp.exp(m - mn)
            p = jnp.exp(s - mn)
            pv = jnp.dot(vt_s[g, :, pl.ds(off, tq)], p.astype(BF16), preferred_element_type=F32)
            m_s[g] = mn
            l_s[g] = alpha * l_s[g] + jnp.sum(p, axis=0, keepdims=True)
            acc_s[g] = alpha * acc_s[g] + pv

    m_s[...] = jnp.full(m_s.shape, NEG, F32)
    l_s[...] = jnp.zeros(l_s.shape, F32)
    acc_s[...] = jnp.zeros(acc_s.shape, F32)
    scores(0, sa_s)

    def pair(jp, carry):
        off = pl.multiple_of(2 * jp * tq, tq)
        scores(off + tq, sb_s)
        consume(off, sa_s, False)
        scores(off + 2 * tq, sa_s)
        consume(off + tq, sb_s, False)
        return carry

    lax.fori_loop(0, qi // 2, pair, 0)

    @pl.when(qi % 2 == 0)
    def _():
        consume(start, sa_s, True)

    @pl.when(qi % 2 == 1)
    def _():
        scores(start, sb_s)
        consume(start - tq, sa_s, False)
        consume(start, sb_s, True)

    lam = _lam(lq1, lk1, lq2, lk2, lam_init)
    for g in range(heads):
        an = acc_s[g] / l_s[g]
        o_t = an[:, :tq] - lam * an[:, tq:]
        ms = jnp.mean(o_t * o_t, axis=0, keepdims=True)
        o_t = o_t * lax.rsqrt(ms + EPS) * sub_ref[...] * (1.0 - lam_init)
        o_ref[0, :, g * hw:(g + 1) * hw] = o_t.T.astype(o_ref.dtype)


def _attn_prompt(q, k, v, lams, subln, lam_init):
    bsz, t, dq = q.shape
    tq = _tile(t, ATT_TQ)
    assert tq % CHUNK == 0
    hw = 2 * ATT_HEAD_DIM
    heads = ATT_HEADS_PER_STEP
    gw = heads * hw
    vec = pl.BlockSpec((1, ATT_HEAD_DIM), lambda b, h, i: (0, 0))
    return pl.pallas_call(
        functools.partial(_attn_prompt_kernel, tq=tq, heads=heads, lam_init=lam_init),
        grid=(bsz, dq // gw, t // tq),
        in_specs=[vec, vec, vec, vec,
                  pl.BlockSpec((hw, 1), lambda b, h, i: (0, 0)),
                  pl.BlockSpec((1, tq, gw), lambda b, h, i: (b, i, h)),
                  pl.BlockSpec((1, t, gw), lambda b, h, i: (b, 0, h)),
                  pl.BlockSpec((1, t, gw), lambda b, h, i: (b, 0, h))],
        out_specs=pl.BlockSpec((1, tq, gw), lambda b, h, i: (b, i, h)),
        out_shape=jax.ShapeDtypeStruct((bsz, t, dq), BF16),
        scratch_shapes=[pltpu.VMEM((heads, hw, t), BF16),
                        pltpu.VMEM((heads, tq, 2 * tq), F32), pltpu.VMEM((heads, tq, 2 * tq), F32),
                        pltpu.VMEM((heads, 1, 2 * tq), F32), pltpu.VMEM((heads, 1, 2 * tq), F32),
                        pltpu.VMEM((heads, hw, 2 * tq), F32)],
        compiler_params=_cp(("parallel", "parallel", "arbitrary")),
        name="attn_prompt",
    )(*lams, subln.reshape(hw, 1), q, k, v)


def _attn_decode_kernel(lq1, lk1, lq2, lk2, sub_ref, q_ref, kc_ref, vc_ref, kn_ref, vn_ref, o_ref, *, lam_init):
    lam = _lam(lq1, lk1, lq2, lk2, lam_init)
    hw = 2 * ATT_HEAD_DIM
    for h in range(q_ref.shape[2] // hw):
        sl = slice(h * hw, (h + 1) * hw)
        q1, q2 = _split_q(q_ref[0, :, sl])
        kc = kc_ref[0, 0, :, sl].astype(BF16)
        vc = vc_ref[0, 0, :, sl].astype(BF16)
        kn = kn_ref[0, :, sl]
        vn = vn_ref[0, :, sl]

        def probs(qc):
            s_c, s_n = _nt(qc, kc), _nt(qc, kn)
            m = jnp.maximum(jnp.max(s_c, axis=-1, keepdims=True), jnp.max(s_n, axis=-1, keepdims=True))
            p_c, p_n = jnp.exp(s_c - m), jnp.exp(s_n - m)
            l = jnp.sum(p_c, axis=-1, keepdims=True) + jnp.sum(p_n, axis=-1, keepdims=True)
            return p_c / l, p_n / l

        p1c, p1n = probs(q1)
        p2c, p2n = probs(q2)
        o = (jnp.dot((p1c - lam * p2c).astype(BF16), vc, preferred_element_type=F32)
             + jnp.dot((p1n - lam * p2n).astype(BF16), vn, preferred_element_type=F32))
        o_ref[0, :, sl] = _subln(o, sub_ref, lam_init).astype(o_ref.dtype)


def _attn_decode(q, kn, vn, cache_k, cache_v, layer, lams, subln, lam_init):
    bsz, t, dq = q.shape
    npast = cache_k.shape[2]
    assert (npast + t - 1) // CHUNK <= npast // CHUNK, "decode kernel assumes every key is visible"
    hw = 2 * ATT_HEAD_DIM
    vec = pl.BlockSpec((1, ATT_HEAD_DIM), lambda b: (0, 0))
    tok = pl.BlockSpec((1, t, dq), lambda b: (b, 0, 0))
    cache = pl.BlockSpec((1, 1, npast, dq), lambda b: (layer, b, 0, 0))
    return pl.pallas_call(
        functools.partial(_attn_decode_kernel, lam_init=lam_init),
        grid=(bsz,),
        in_specs=[vec, vec, vec, vec, pl.BlockSpec((1, hw), lambda b: (0, 0)), tok, cache, cache, tok, tok],
        out_specs=tok,
        out_shape=jax.ShapeDtypeStruct((bsz, t, dq), BF16),
        compiler_params=_cp(("parallel",)),
        name="attn_decode",
    )(*lams, subln.reshape(1, hw), q, cache_k, cache_v, kn, vn)


def _merge_kernel(ys_ref, o_ref, wa_ref, wb_ref, g1_ref, g2_ref, b1_ref, b2_ref, out_ref):
    a = jnp.dot(ys_ref[...], wa_ref[...], preferred_element_type=F32)
    b = jnp.dot(o_ref[...], wb_ref[...], preferred_element_type=F32)
    g1 = _sigmoid(g1_ref[...] + b1_ref[...])
    g2 = _sigmoid(g2_ref[...] + b2_ref[...])
    out_ref[...] = (g1 * a + g2 * b).astype(out_ref.dtype)


def _merge(ys, o, w_ssm_out, w_att_out, g, b_gate):
    m, ka = ys.shape
    kb = o.shape[1]
    d = w_ssm_out.shape[1]
    tm, tn = _tile(m, 1024), _tile(d, 512)
    nb = d // tn
    bg = b_gate.reshape(1, 2 * d)
    return pl.pallas_call(
        _merge_kernel,
        grid=(m // tm, nb),
        in_specs=[pl.BlockSpec((tm, ka), lambda i, j: (i, 0)),
                  pl.BlockSpec((tm, kb), lambda i, j: (i, 0)),
                  pl.BlockSpec((ka, tn), lambda i, j: (0, j)),
                  pl.BlockSpec((kb, tn), lambda i, j: (0, j)),
                  pl.BlockSpec((tm, tn), lambda i, j: (i, j)),
                  pl.BlockSpec((tm, tn), lambda i, j: (i, j + nb)),
                  pl.BlockSpec((1, tn), lambda i, j: (0, j)),
                  pl.BlockSpec((1, tn), lambda i, j: (0, j + nb))],
        out_specs=pl.BlockSpec((tm, tn), lambda i, j: (i, j)),
        out_shape=jax.ShapeDtypeStruct((m, d), BF16),
        compiler_params=_cp(("parallel", "arbitrary")),
        name="merge",
    )(ys, o, w_ssm_out, w_att_out, g, g, bg, bg)


def _rope_lane_tables(pos):
    half = ROPE_DIM // 2
    inv = 1.0 / (ROPE_THETA ** (jnp.arange(0, ROPE_DIM, 2, dtype=F32) / ROPE_DIM))
    ang = pos.astype(F32)[:, None] * inv[None, :]
    cos, sin = jnp.cos(ang), jnp.sin(ang)
    d = jnp.arange(LANES) % ATT_HEAD_DIM
    lo, hi = d < half, (d >= half) & (d < ROPE_DIM)
    idx = jnp.where(hi, d - half, jnp.where(lo, d, 0))
    c = jnp.where((lo | hi)[None, :], cos[:, idx], 1.0)
    sp = jnp.where(hi[None, :], sin[:, idx], 0.0)
    sm = jnp.where(lo[None, :], -sin[:, idx], 0.0)
    return c, sp, sm


def _split_w_in(w_in, d_model):
    ds = SSM_HEADS * SSM_HEAD_DIM
    cdim = ds + 2 * SSM_GROUPS * SSM_STATE
    dqk = ATT_HEADS * 2 * ATT_HEAD_DIM
    edges = [0, ds, ds + cdim, ds + cdim + SSM_HEADS]
    edges += [edges[-1] + dqk, edges[-1] + 2 * dqk, edges[-1] + 3 * dqk, edges[-1] + 3 * dqk + 2 * d_model]
    parts = [w_in[:, :, a:b].astype(BF16) for a, b in zip(edges[:-1], edges[1:])]
    parts[2] = jnp.pad(parts[2], ((0, 0), (0, 0), (0, LANES - SSM_HEADS)))
    return parts


def _trunk(x, pos, cache_k, cache_v, state_conv, state_ssm, p):
    bsz, t, d = x.shape
    m = bsz * t
    depth = p["w13_ffn1"].shape[0]
    decode = cache_k is not None
    ds = SSM_HEADS * SSM_HEAD_DIM
    cdim = ds + 2 * SSM_GROUPS * SSM_STATE
    dqk = ATT_HEADS * 2 * ATT_HEAD_DIM
    tabs = _rope_lane_tables(pos)
    if decode:
        tabs = tuple(jnp.tile(tb, (bsz, 1)) for tb in tabs)
        period = m
    else:
        period = t
    t_pad = -(-t // SSD_L) * SSD_L
    dt_bias = jnp.pad(p["dt_bias"].astype(F32), ((0, 0), (0, LANES - SSM_HEADS)))

    x = x.reshape(m, d)
    xn = _rmsnorm(x, p["norm_ffn1"][0])
    kstack = vstack = None
    convs, ssms = [], []
    y = None
    for i in range(depth):
        lam_init = 0.8 - 0.6 * math.exp(-0.3 * i)
        h = _swiglu_up(xn, p["w13_ffn1"][i])
        x, xn = _resid_norm(h, p["w2_ffn1"][i], x, p["norm_mix"][i], 0.5)
        (z,) = _proj(xn, p["w_z"][i], (F32,))
        (xbc,) = _proj(xn, p["w_xbc"][i], (F32,))
        (dt,) = _proj(xn, p["w_dt"][i], (F32,), kind="dt", extra=(dt_bias[i:i + 1],))
        (q,) = _proj(xn, p["w_q"][i], (BF16,), kind="rope", extra=tabs, scale=ATT_HEAD_DIM ** -0.5, period=period)
        kstack, k16 = _proj(xn, p["w_k"][i], (F32, BF16), kind="rope", extra=tabs, period=period,
                            stack=(kstack, i, depth))
        vstack, v16 = _proj(xn, p["w_v"][i], (F32, BF16), stack=(vstack, i, depth))
        (g,) = _proj(xn, p["w_g"][i], (F32,))
        xbc3 = xbc.reshape(bsz, t, cdim)
        dt3 = dt.reshape(bsz, t, LANES)
        z3 = z.reshape(bsz, t, ds)
        if t_pad != t:
            padt = ((0, 0), (0, t_pad - t), (0, 0))
            xbc_in, dt_in, z_in = jnp.pad(xbc3, padt), jnp.pad(dt3, padt), jnp.pad(z3, padt)
        else:
            xbc_in, dt_in, z_in = xbc3, dt3, z3
        if decode:
            tail0 = jnp.pad(state_conv[i].astype(F32), ((0, 0), (SUBLANES - (D_CONV - 1), 0), (0, 0)))
            h0 = jnp.transpose(state_ssm[i].astype(F32).reshape(bsz, ds, SSM_STATE), (0, 2, 1))
            prev = state_conv[i].astype(F32)
        else:
            tail0 = h0 = None
            prev = jnp.zeros((bsz, D_CONV - 1, cdim), F32)
        ys, h_t = _ssd(xbc_in, dt_in, z_in, p["conv_w"][i], p["conv_b"][i], p["a_log"][i], p["d_skip"][i],
                       p["ssm_norm"][i], tail0, h0)
        ys = ys[:, :t].reshape(m, ds)
        convs.append(jnp.concatenate([prev, xbc3], axis=1)[:, t:])
        ssms.append(jnp.transpose(h_t, (0, 2, 1)).reshape(bsz, SSM_HEADS, SSM_HEAD_DIM, SSM_STATE))
        lams = tuple(p[n][i].reshape(1, ATT_HEAD_DIM).astype(F32)
                     for n in ("lambda_q1", "lambda_k1", "lambda_q2", "lambda_k2"))
        q3, k3, v3 = (a.reshape(bsz, t, dqk) for a in (q, k16, v16))
        if decode:
            o = _attn_decode(q3, k3, v3, cache_k, cache_v, i, lams, p["subln"][i], lam_init)
        else:
            o = _attn_prompt(q3, k3, v3, lams, p["subln"][i], lam_init)
        mix = _merge(ys, o.reshape(m, dqk), p["w_ssm_out"][i], p["w_att_out"][i], g, p["b_gate"][i])
        x, xn = _resid_norm(mix, p["w_o"][i], x, p["norm_ffn2"][i], 1.0)
        h = _swiglu_up(xn, p["w13_ffn2"][i])
        if i + 1 < depth:
            x, xn = _resid_norm(h, p["w2_ffn2"][i], x, p["norm_ffn1"][i + 1], 0.5)
        else:
            y = _resid_norm(h, p["w2_ffn2"][i], x, p["norm_final"], 0.5, final=True)
    k_all = kstack.reshape(depth, bsz, t, ATT_HEADS, 2, ATT_HEAD_DIM)
    v_all = vstack.reshape(depth, bsz, t, ATT_HEADS, 2 * ATT_HEAD_DIM)
    return y.reshape(bsz, t, d), k_all, v_all, jnp.stack(convs), jnp.stack(ssms)


def kernel(x_prompt, x_sample, cache_k, cache_v, state_conv, state_ssm, norm_ffn1, w13_ffn1, w2_ffn1, norm_mix, w_in, conv_w, conv_b, dt_bias, a_log, d_skip, ssm_norm, w_ssm_out, lambda_q1, lambda_k1, lambda_q2, lambda_k2, subln, w_att_out, b_gate, w_o, norm_ffn2, w13_ffn2, w2_ffn2, norm_final):
    d_model = x_prompt.shape[-1]
    w_z, w_xbc, w_dt, w_q, w_k, w_v, w_g = _split_w_in(w_in, d_model)
    p = dict(
        norm_ffn1=norm_ffn1, w13_ffn1=w13_ffn1.astype(BF16), w2_ffn1=w2_ffn1.astype(BF16), norm_mix=norm_mix,
        w_z=w_z, w_xbc=w_xbc, w_dt=w_dt, w_q=w_q, w_k=w_k, w_v=w_v, w_g=w_g,
        conv_w=conv_w, conv_b=conv_b, dt_bias=dt_bias, a_log=a_log, d_skip=d_skip, ssm_norm=ssm_norm,
        w_ssm_out=w_ssm_out.astype(BF16), lambda_q1=lambda_q1, lambda_k1=lambda_k1, lambda_q2=lambda_q2,
        lambda_k2=lambda_k2, subln=subln, w_att_out=w_att_out.astype(BF16), b_gate=b_gate,
        w_o=w_o.astype(BF16), norm_ffn2=norm_ffn2, w13_ffn2=w13_ffn2.astype(BF16), w2_ffn2=w2_ffn2.astype(BF16),
        norm_final=norm_final)
    depth, dec_b, n_past = cache_k.shape[:3]
    dqk = ATT_HEADS * 2 * ATT_HEAD_DIM
    y_p, k_p, v_p, conv_p, ssm_p = _trunk(x_prompt, jnp.arange(x_prompt.shape[1]), None, None, None, None, p)
    ck = cache_k.reshape(depth, dec_b, n_past, dqk)
    cv = cache_v.reshape(depth, dec_b, n_past, dqk)
    y_s, k_s, v_s, conv_s, ssm_s = _trunk(x_sample, n_past + jnp.arange(x_sample.shape[1]), ck, cv,
                                          state_conv, state_ssm, p)
    return (y_p, y_s, k_p, v_p, conv_p, ssm_p, k_s, v_s, conv_s, ssm_s)
```

```python
import functools
import math

import jax
import jax.numpy as jnp
from jax import lax
from jax.experimental import pallas as pl
from jax.experimental.pallas import tpu as pltpu

F32 = jnp.float32
BF16 = jnp.bfloat16

CHUNK = 64
SSM_HEADS = 32
SSM_HEAD_DIM = 64
SSM_GROUPS = 8
SSM_STATE = 128
D_CONV = 4
ATT_HEADS = 16
ATT_HEAD_DIM = 64
ROPE_DIM = ATT_HEAD_DIM // 4
ROPE_THETA = 500000.0
EPS = 1e-6

LANES = 128
SUBLANES = 8
SSD_L = 128
ATT_TQ = 512
ATT_TK = 256
ATT_HEADS_PER_STEP = 2
VMEM_LIMIT = 56 * 1024 * 1024
NEG = -1e30


def _cp(sem, vmem=VMEM_LIMIT):
    return pltpu.CompilerParams(dimension_semantics=sem, vmem_limit_bytes=vmem)


def _tile(dim, pref):
    t = min(dim, pref)
    while dim % t:
        t //= 2
    return t


def _sigmoid(x):
    return 1.0 / (1.0 + jnp.exp(-x))


def _nt(a, b):
    return lax.dot_general(a, b, (((1,), (1,)), ((), ())), preferred_element_type=F32)


def _tn(a, b):
    return lax.dot_general(a, b, (((0,), (0,)), ((), ())), preferred_element_type=F32)


def _rmsnorm_kernel(x_ref, g_ref, o_ref):
    x = x_ref[...]
    ms = jnp.mean(x * x, axis=-1, keepdims=True)
    o_ref[...] = (x * lax.rsqrt(ms + EPS) * g_ref[...]).astype(o_ref.dtype)


def _rmsnorm(x, g):
    m, d = x.shape
    tm = _tile(m, 512)
    return pl.pallas_call(
        _rmsnorm_kernel,
        grid=(m // tm,),
        in_specs=[pl.BlockSpec((tm, d), lambda i: (i, 0)), pl.BlockSpec((1, d), lambda i: (0, 0))],
        out_specs=pl.BlockSpec((tm, d), lambda i: (i, 0)),
        out_shape=jax.ShapeDtypeStruct((m, d), BF16),
        compiler_params=_cp(("parallel",)),
        name="rmsnorm",
    )(x, g.reshape(1, d))


def _swiglu_kernel(x_ref, w1_ref, w3_ref, o_ref):
    x = x_ref[...]
    a = jnp.dot(x, w1_ref[...], preferred_element_type=F32)
    b = jnp.dot(x, w3_ref[...], preferred_element_type=F32)
    o_ref[...] = (a * _sigmoid(a) * b).astype(o_ref.dtype)


def _swiglu_up(xn, w13):
    m, d = xn.shape
    f = w13.shape[1] // 2
    tm, tn = _tile(m, 1024), _tile(f, 512)
    nb = f // tn
    return pl.pallas_call(
        _swiglu_kernel,
        grid=(m // tm, nb),
        in_specs=[pl.BlockSpec((tm, d), lambda i, j: (i, 0)),
                  pl.BlockSpec((d, tn), lambda i, j: (0, j)),
                  pl.BlockSpec((d, tn), lambda i, j: (0, j + nb))],
        out_specs=pl.BlockSpec((tm, tn), lambda i, j: (i, j)),
        out_shape=jax.ShapeDtypeStruct((m, f), BF16),
        compiler_params=_cp(("parallel", "arbitrary")),
        name="swiglu_up",
    )(xn, w13, w13)


def _resid_norm_kernel(h_ref, w_ref, x_ref, g_ref, *o_refs, scale, final):
    acc = jnp.dot(h_ref[...], w_ref[...], preferred_element_type=F32)
    xnew = x_ref[...] + scale * acc
    ms = jnp.mean(xnew * xnew, axis=-1, keepdims=True)
    y = xnew * lax.rsqrt(ms + EPS) * g_ref[...]
    if final:
        o_refs[0][...] = y
    else:
        o_refs[0][...] = xnew
        o_refs[1][...] = y.astype(BF16)


def _resid_norm(h, w, x, g, scale, final=False):
    m, k = h.shape
    d = w.shape[1]
    tm = _tile(m, 256)
    row = pl.BlockSpec((tm, d), lambda i: (i, 0))
    if final:
        out_shape, out_specs = jax.ShapeDtypeStruct((m, d), F32), row
    else:
        out_shape = (jax.ShapeDtypeStruct((m, d), F32), jax.ShapeDtypeStruct((m, d), BF16))
        out_specs = (row, row)
    return pl.pallas_call(
        functools.partial(_resid_norm_kernel, scale=scale, final=final),
        grid=(m // tm,),
        in_specs=[pl.BlockSpec((tm, k), lambda i: (i, 0)),
                  pl.BlockSpec((k, d), lambda i: (0, 0), pipeline_mode=pl.Buffered(1)),
                  row,
                  pl.BlockSpec((1, d), lambda i: (0, 0))],
        out_specs=out_specs,
        out_shape=out_shape,
        compiler_params=_cp(("parallel",)),
        name="resid_norm",
    )(h, w, x, g.reshape(1, d))


def _proj_plain_kernel(x_ref, w_ref, *o_refs):
    acc = jnp.dot(x_ref[...], w_ref[...], preferred_element_type=F32)
    for o in o_refs:
        o[...] = acc.astype(o.dtype)


def _proj_dt_kernel(x_ref, w_ref, b_ref, o_ref):
    v = jnp.dot(x_ref[...], w_ref[...], preferred_element_type=F32) + b_ref[...]
    o_ref[...] = jnp.maximum(v, 0.0) + jnp.log1p(jnp.exp(-jnp.abs(v)))


def _proj_rope_kernel(x_ref, w_ref, c_ref, sp_ref, sm_ref, *o_refs, scale, transposed0):
    acc = jnp.dot(x_ref[...], w_ref[...], preferred_element_type=F32)
    c, sp, sm = c_ref[...], sp_ref[...], sm_ref[...]
    half = ROPE_DIM // 2
    for cb in range(acc.shape[1] // LANES):
        sl = slice(cb * LANES, (cb + 1) * LANES)
        y = acc[:, sl]
        r = y * c + pltpu.roll(y, half, 1) * sp + pltpu.roll(y, LANES - half, 1) * sm
        if scale != 1.0:
            r = r * scale
        for n, o in enumerate(o_refs):
            if n == 0 and transposed0:
                o[sl, :] = r.T.astype(o.dtype)
            else:
                o[:, sl] = r.astype(o.dtype)


def _proj(xn, w, out_dtypes, kind="plain", extra=(), scale=1.0, period=None, stack=None, transposed0=False):
    m, d = xn.shape
    n = w.shape[1]
    tm, tn = _tile(m, 1024), _tile(n, 1024)
    if kind == "rope":
        tm = _tile(period, tm)
    in_specs = [pl.BlockSpec((tm, d), lambda i, j: (i, 0)), pl.BlockSpec((d, tn), lambda i, j: (0, j))]
    if kind == "plain":
        body = _proj_plain_kernel
    elif kind == "dt":
        body = _proj_dt_kernel
        in_specs.append(pl.BlockSpec((1, tn), lambda i, j: (0, j)))
    else:
        body = functools.partial(_proj_rope_kernel, scale=scale, transposed0=transposed0)
        nper = period // tm
        in_specs += [pl.BlockSpec((tm, LANES), lambda i, j: (i % nper, 0))] * 3
    out_shape = [jax.ShapeDtypeStruct((m, n), dt) for dt in out_dtypes]
    out_specs = [pl.BlockSpec((tm, tn), lambda i, j: (i, j)) for _ in out_dtypes]
    args = [xn, w, *extra]
    aliases = {}
    if stack is not None:
        buf, layer, depth = stack
        if transposed0:
            out_shape[0] = jax.ShapeDtypeStruct((depth, m // period, n, period), out_dtypes[0])
            out_specs[0] = pl.BlockSpec((None, None, tn, tm), lambda i, j: (layer, i // nper, j, i % nper))
        else:
            out_shape[0] = jax.ShapeDtypeStruct((depth, m, n), out_dtypes[0])
            out_specs[0] = pl.BlockSpec((None, tm, tn), lambda i, j: (layer, i, j))
        if buf is not None:
            n_in = len(args)
            inner = body
            body = lambda *refs: inner(*refs[:n_in], *refs[n_in + 1:])
            in_specs.append(pl.BlockSpec(memory_space=pl.ANY))
            args.append(buf)
            aliases = {n_in: 0}
    return pl.pallas_call(
        body,
        grid=(m // tm, n // tn),
        in_specs=in_specs,
        out_specs=tuple(out_specs),
        out_shape=tuple(out_shape),
        input_output_aliases=aliases,
        compiler_params=_cp(("parallel", "arbitrary")),
        name="proj_" + kind,
    )(*args)


def _ssd_kernel(*refs, has_state):
    if has_state:
        (xbc_ref, dt_ref, z_ref, cw_ref, cb_ref, alog_ref, dsk_ref, nw_ref, tail0_ref, h0_ref,
         y_ref, hout_ref, tail_s, h_s, act_s, y_s) = refs
    else:
        (xbc_ref, dt_ref, z_ref, cw_ref, cb_ref, alog_ref, dsk_ref, nw_ref,
         y_ref, hout_ref, tail_s, h_s, act_s, y_s) = refs
    L, N, P = SSD_L, SSM_STATE, SSM_HEAD_DIM
    ds = SSM_HEADS * P
    cdim = ds + 2 * SSM_GROUPS * N
    c = pl.program_id(1)

    @pl.when(c == 0)
    def _():
        if has_state:
            tail_s[...] = tail0_ref[0]
            h_s[...] = h0_ref[0]
        else:
            tail_s[...] = jnp.zeros_like(tail_s)
            h_s[...] = jnp.zeros_like(h_s)

    cw = 512
    row8 = lax.broadcasted_iota(jnp.int32, (SUBLANES, cw), 0)
    for cb in range(cdim // cw):
        sl = slice(cb * cw, (cb + 1) * cw)
        x_top = xbc_ref[0, :SUBLANES, sl]
        tail8 = tail_s[:, sl]
        top = cb_ref[:, sl] + x_top * cw_ref[D_CONV - 1:D_CONV, sl]
        rest = cb_ref[:, sl] + xbc_ref[0, SUBLANES:, sl] * cw_ref[D_CONV - 1:D_CONV, sl]
        for k in range(1, D_CONV):
            wk = cw_ref[D_CONV - 1 - k:D_CONV - k, sl]
            first = jnp.where(row8 < k, pltpu.roll(tail8, k, 0), pltpu.roll(x_top, k, 0))
            top = top + first * wk
            rest = rest + xbc_ref[0, SUBLANES - k:L - k, sl] * wk
        tail_s[:, sl] = xbc_ref[0, L - SUBLANES:, sl]
        conv = jnp.concatenate([top, rest], axis=0)
        act_s[:, sl] = conv * _sigmoid(conv)

    dtv = dt_ref[0]
    da = dtv * (-jnp.exp(alog_ref[...]))
    ri = lax.broadcasted_iota(jnp.int32, (L, L), 0)
    ci = lax.broadcasted_iota(jnp.int32, (L, L), 1)
    tri = ri >= ci
    acs = jnp.dot(tri.astype(F32), da, precision=lax.Precision.HIGHEST, preferred_element_type=F32)
    acs_t = acs.T
    dt_t = dtv.T
    acs_end = acs[L - 1:L, :]
    lane = lax.broadcasted_iota(jnp.int32, (L, LANES), 1)
    lane1 = lax.broadcasted_iota(jnp.int32, (1, LANES), 1)
    heads_per_group = SSM_HEADS // SSM_GROUPS
    pairs_per_group = heads_per_group // 2

    for g in range(SSM_GROUPS):
        bg = act_s[:, ds + g * N: ds + (g + 1) * N]
        cg = act_s[:, ds + (SSM_GROUPS + g) * N: ds + (SSM_GROUPS + g + 1) * N]
        sc = _nt(cg.astype(BF16), bg.astype(BF16))
        for pr in range(pairs_per_group):
            pair = g * pairs_per_group + pr
            psl = slice(pair * LANES, (pair + 1) * LANES)
            xs_pair = act_s[:, psl]
            ht_pair = h_s[:, psl]
            ypair = jnp.zeros((L, LANES), F32)
            upd = jnp.zeros((N, LANES), F32)
            dec_lane = jnp.zeros((1, LANES), F32)
            for e in range(2):
                h = pair * 2 + e
                half = (lane >= e * P) & (lane < (e + 1) * P)
                colb = jnp.broadcast_to(acs[:, h:h + 1], (L, LANES))
                dtcolb = jnp.broadcast_to(dtv[:, h:h + 1], (L, LANES))
                rowb = acs_t[h:h + 1, :]
                dtrow = dt_t[h:h + 1, :]
                dec = jnp.exp(jnp.where(tri, colb - rowb, NEG))
                mh = (sc * dec * dtrow).astype(BF16)
                ce = (cg * jnp.exp(colb)).astype(BF16)
                xs_m = jnp.where(half, xs_pair, 0.0).astype(BF16)
                ht_m = jnp.where(half, ht_pair, 0.0).astype(BF16)
                lhs = jnp.concatenate([mh, ce], axis=1)
                rhs = jnp.concatenate([xs_m, ht_m], axis=0)
                ypair = ypair + jnp.dot(lhs, rhs, preferred_element_type=F32)
                endb = acs_end[:, h:h + 1]
                bw = (bg * (dtcolb * jnp.exp(endb - colb))).astype(BF16)
                upd = upd + _tn(bw, xs_m)
                half1 = (lane1 >= e * P) & (lane1 < (e + 1) * P)
                dec_lane = jnp.where(half1, jnp.exp(endb), dec_lane)
            h_s[:, psl] = ht_pair * dec_lane + upd
            y_s[:, psl] = ypair

    gs = ds // SSM_GROUPS
    for g in range(SSM_GROUPS):
        sl = slice(g * gs, (g + 1) * gs)
        z = z_ref[0, :, sl]
        y = (y_s[:, sl] + dsk_ref[:, sl] * act_s[:, sl]) * (z * _sigmoid(z))
        ms = jnp.mean(y * y, axis=-1, keepdims=True)
        y_ref[0, :, sl] = (y * lax.rsqrt(ms + EPS) * nw_ref[:, sl]).astype(y_ref.dtype)

    @pl.when(c == pl.num_programs(1) - 1)
    def _():
        hout_ref[0] = h_s[...]


def _ssd(xbc, dt, z, conv_w, conv_b, a_log, d_skip, ssm_norm, tail0=None, h0=None):
    bsz, t, cdim = xbc.shape
    ds = SSM_HEADS * SSM_HEAD_DIM
    L = SSD_L
    assert SSD_L == SSM_STATE == LANES and t % L == 0
    has_state = h0 is not None
    pad = LANES - SSM_HEADS
    alog = jnp.pad(a_log.astype(F32), (0, pad)).reshape(1, LANES)
    dsk = jnp.repeat(d_skip.astype(F32), SSM_HEAD_DIM).reshape(1, ds)
    const = lambda shape: pl.BlockSpec(shape, lambda b, c: (0,) * len(shape))
    in_specs = [pl.BlockSpec((1, L, cdim), lambda b, c: (b, c, 0)),
                pl.BlockSpec((1, L, LANES), lambda b, c: (b, c, 0)),
                pl.BlockSpec((1, L, ds), lambda b, c: (b, c, 0)),
                const((D_CONV, cdim)), const((1, cdim)), const((1, LANES)), const((1, ds)), const((1, ds))]
    args = [xbc, dt, z, conv_w, conv_b.reshape(1, cdim), alog, dsk, ssm_norm.reshape(1, ds)]
    if has_state:
        in_specs += [pl.BlockSpec((1, SUBLANES, cdim), lambda b, c: (b, 0, 0)),
                     pl.BlockSpec((1, SSM_STATE, ds), lambda b, c: (b, 0, 0))]
        args += [tail0, h0]
    return pl.pallas_call(
        functools.partial(_ssd_kernel, has_state=has_state),
        grid=(bsz, t // L),
        in_specs=in_specs,
        out_specs=(pl.BlockSpec((1, L, ds), lambda b, c: (b, c, 0)),
                   pl.BlockSpec((1, SSM_STATE, ds), lambda b, c: (b, 0, 0))),
        out_shape=(jax.ShapeDtypeStruct((bsz, t, ds), BF16),
                   jax.ShapeDtypeStruct((bsz, SSM_STATE, ds), F32)),
        scratch_shapes=[pltpu.VMEM((SUBLANES, cdim), F32), pltpu.VMEM((SSM_STATE, ds), F32),
                        pltpu.VMEM((L, cdim), F32), pltpu.VMEM((L, ds), F32)],
        compiler_params=_cp(("parallel", "arbitrary")),
        name="ssd",
    )(*args)


def _lam(lq1, lk1, lq2, lk2, lam_init):
    return (jnp.exp(jnp.sum(lq1[...] * lk1[...], axis=-1, keepdims=True))
            - jnp.exp(jnp.sum(lq2[...] * lk2[...], axis=-1, keepdims=True)) + lam_init)


def _split_q(q):
    lane = lax.broadcasted_iota(jnp.int32, q.shape, 1)
    zero = jnp.zeros_like(q)
    return jnp.where(lane < ATT_HEAD_DIM, q, zero), jnp.where(lane >= ATT_HEAD_DIM, q, zero)


def _subln(o, sub_ref, lam_init):
    ms = jnp.mean(o * o, axis=-1, keepdims=True)
    return o * lax.rsqrt(ms + EPS) * sub_ref[...] * (1.0 - lam_init)


def _attn_prompt_kernel(lq1, lk1, lq2, lk2, sub_ref, q_ref, k_ref, v_ref, o_ref,
                        vt_s, sa_s, sb_s, m_s, l_s, acc_s, *, tq, tk, heads, lam_init):
    qi = pl.program_id(2)
    hw = 2 * ATT_HEAD_DIM
    t = k_ref.shape[1]

    @pl.when(qi == 0)
    def _():
        for g in range(heads):
            for cb in range(t // tk):
                rows = slice(cb * tk, (cb + 1) * tk)
                vt_s[g, :, rows] = v_ref[0, rows, g * hw:(g + 1) * hw].astype(F32).T.astype(BF16)

    start = pl.multiple_of(qi * tq, tq)
    kr = lax.broadcasted_iota(jnp.int32, (tk, 2 * tq), 0)
    qc = lax.broadcasted_iota(jnp.int32, (tk, 2 * tq), 1)
    qc = jnp.where(qc >= tq, qc - tq, qc)
    qq = []
    for g in range(heads):
        q1, q2 = _split_q(q_ref[0, :, g * hw:(g + 1) * hw])
        qq.append(jnp.concatenate([q1, q2], axis=0))

    def scores(off, s_ref):
        for g in range(heads):
            s_ref[g] = _nt(k_ref[0, pl.ds(off, tk), g * hw:(g + 1) * hw], qq[g])

    def consume(off, s_ref, diag):
        for g in range(heads):
            s = s_ref[g]
            if diag is not None:
                s = jnp.where(((kr + diag * tk) // CHUNK) <= (qc // CHUNK), s, NEG)
            m = m_s[g]
            mn = jnp.maximum(m, jnp.max(s, axis=0, keepdims=True))
            alpha = jnp.exp(m - mn)
            p = jnp.exp(s - mn)
            pv = jnp.dot(vt_s[g, :, pl.ds(off, tk)], p.astype(BF16), preferred_element_type=F32)
            m_s[g] = mn
            l_s[g] = alpha * l_s[g] + jnp.sum(p, axis=0, keepdims=True)
            acc_s[g] = alpha * acc_s[g] + pv

    m_s[...] = jnp.full(m_s.shape, NEG, F32)
    l_s[...] = jnp.zeros(l_s.shape, F32)
    acc_s[...] = jnp.zeros(acc_s.shape, F32)
    scores(0, sa_s)

    def pair(jp, carry):
        off = pl.multiple_of(2 * jp * tk, tq)
        scores(off + tk, sb_s)
        consume(off, sa_s, None)
        scores(off + 2 * tk, sa_s)
        consume(off + tk, sb_s, None)
        return carry

    lax.fori_loop(0, qi, pair, 0)
    scores(start + tk, sb_s)
    consume(start, sa_s, 0)
    consume(start + tk, sb_s, 1)

    lam = _lam(lq1, lk1, lq2, lk2, lam_init)
    for g in range(heads):
        an = acc_s[g] / l_s[g]
        o_t = an[:, :tq] - lam * an[:, tq:]
        ms = jnp.mean(o_t * o_t, axis=0, keepdims=True)
        o_t = o_t * lax.rsqrt(ms + EPS) * sub_ref[...] * (1.0 - lam_init)
        o_ref[0, :, g * hw:(g + 1) * hw] = o_t.T.astype(o_ref.dtype)


def _attn_prompt(q, k, v, lams, subln, lam_init):
    bsz, t, dq = q.shape
    tq, tk = ATT_TQ, ATT_TK
    assert t % tq == 0 and tq == 2 * tk and tk % CHUNK == 0
    hw = 2 * ATT_HEAD_DIM
    heads = ATT_HEADS_PER_STEP
    gw = heads * hw
    vec = pl.BlockSpec((1, ATT_HEAD_DIM), lambda b, h, i: (0, 0))
    return pl.pallas_call(
        functools.partial(_attn_prompt_kernel, tq=tq, tk=tk, heads=heads, lam_init=lam_init),
        grid=(bsz, dq // gw, t // tq),
        in_specs=[vec, vec, vec, vec,
                  pl.BlockSpec((hw, 1), lambda b, h, i: (0, 0)),
                  pl.BlockSpec((1, tq, gw), lambda b, h, i: (b, i, h)),
                  pl.BlockSpec((1, t, gw), lambda b, h, i: (b, 0, h)),
                  pl.BlockSpec((1, t, gw), lambda b, h, i: (b, 0, h))],
        out_specs=pl.BlockSpec((1, tq, gw), lambda b, h, i: (b, i, h)),
        out_shape=jax.ShapeDtypeStruct((bsz, t, dq), BF16),
        scratch_shapes=[pltpu.VMEM((heads, hw, t), BF16),
                        pltpu.VMEM((heads, tk, 2 * tq), F32), pltpu.VMEM((heads, tk, 2 * tq), F32),
                        pltpu.VMEM((heads, 1, 2 * tq), F32), pltpu.VMEM((heads, 1, 2 * tq), F32),
                        pltpu.VMEM((heads, hw, 2 * tq), F32)],
        compiler_params=_cp(("parallel", "parallel", "arbitrary")),
        name="attn_prompt",
    )(*lams, subln.reshape(hw, 1), q, k, v)


def _attn_decode_kernel(lq1, lk1, lq2, lk2, sub_ref, q_ref, kc_ref, vc_ref, kn_ref, vn_ref, o_ref, *, lam_init):
    lam = _lam(lq1, lk1, lq2, lk2, lam_init)
    hw = 2 * ATT_HEAD_DIM
    for h in range(q_ref.shape[2] // hw):
        sl = slice(h * hw, (h + 1) * hw)
        q1, q2 = _split_q(q_ref[0, :, sl])
        kct = kc_ref[0, 0, sl, :].astype(BF16)
        vc = vc_ref[0, 0, :, h, :].astype(BF16)
        kn = kn_ref[0, :, sl]
        vn = vn_ref[0, :, sl]

        def probs(qc):
            s_c, s_n = jnp.dot(qc, kct, preferred_element_type=F32), _nt(qc, kn)
            m = jnp.maximum(jnp.max(s_c, axis=-1, keepdims=True), jnp.max(s_n, axis=-1, keepdims=True))
            p_c, p_n = jnp.exp(s_c - m), jnp.exp(s_n - m)
            l = jnp.sum(p_c, axis=-1, keepdims=True) + jnp.sum(p_n, axis=-1, keepdims=True)
            return p_c / l, p_n / l

        p1c, p1n = probs(q1)
        p2c, p2n = probs(q2)
        o = (jnp.dot((p1c - lam * p2c).astype(BF16), vc, preferred_element_type=F32)
             + jnp.dot((p1n - lam * p2n).astype(BF16), vn, preferred_element_type=F32))
        o_ref[0, :, sl] = _subln(o, sub_ref, lam_init).astype(o_ref.dtype)


def _attn_decode(q, kn, vn, cache_k, cache_v, layer, lams, subln, lam_init):
    bsz, t, dq = q.shape
    npast = cache_v.shape[2]
    assert (npast + t - 1) // CHUNK <= npast // CHUNK, "decode kernel assumes every key is visible"
    hw = 2 * ATT_HEAD_DIM
    vec = pl.BlockSpec((1, ATT_HEAD_DIM), lambda b: (0, 0))
    tok = pl.BlockSpec((1, t, dq), lambda b: (b, 0, 0))
    cache = pl.BlockSpec((1, 1, npast, dq // hw, hw), lambda b: (layer, b, 0, 0, 0))
    cache_t = pl.BlockSpec((1, 1, dq, npast), lambda b: (layer, b, 0, 0))
    return pl.pallas_call(
        functools.partial(_attn_decode_kernel, lam_init=lam_init),
        grid=(bsz,),
        in_specs=[vec, vec, vec, vec, pl.BlockSpec((1, hw), lambda b: (0, 0)), tok, cache_t, cache, tok, tok],
        out_specs=tok,
        out_shape=jax.ShapeDtypeStruct((bsz, t, dq), BF16),
        compiler_params=_cp(("parallel",)),
        name="attn_decode",
    )(*lams, subln.reshape(1, hw), q, cache_k, cache_v, kn, vn)


def _merge_kernel(ys_ref, o_ref, wa_ref, wb_ref, g1_ref, g2_ref, b1_ref, b2_ref, out_ref):
    a = jnp.dot(ys_ref[...], wa_ref[...], preferred_element_type=F32)
    b = jnp.dot(o_ref[...], wb_ref[...], preferred_element_type=F32)
    g1 = _sigmoid(g1_ref[...] + b1_ref[...])
    g2 = _sigmoid(g2_ref[...] + b2_ref[...])
    out_ref[...] = (g1 * a + g2 * b).astype(out_ref.dtype)


def _merge(ys, o, w_ssm_out, w_att_out, g, b_gate):
    m, ka = ys.shape
    kb = o.shape[1]
    d = w_ssm_out.shape[1]
    tm, tn = _tile(m, 1024), _tile(d, 512)
    nb = d // tn
    bg = b_gate.reshape(1, 2 * d)
    return pl.pallas_call(
        _merge_kernel,
        grid=(m // tm, nb),
        in_specs=[pl.BlockSpec((tm, ka), lambda i, j: (i, 0)),
                  pl.BlockSpec((tm, kb), lambda i, j: (i, 0)),
                  pl.BlockSpec((ka, tn), lambda i, j: (0, j)),
                  pl.BlockSpec((kb, tn), lambda i, j: (0, j)),
                  pl.BlockSpec((tm, tn), lambda i, j: (i, j)),
                  pl.BlockSpec((tm, tn), lambda i, j: (i, j + nb)),
                  pl.BlockSpec((1, tn), lambda i, j: (0, j)),
                  pl.BlockSpec((1, tn), lambda i, j: (0, j + nb))],
        out_specs=pl.BlockSpec((tm, tn), lambda i, j: (i, j)),
        out_shape=jax.ShapeDtypeStruct((m, d), BF16),
        compiler_params=_cp(("parallel", "arbitrary")),
        name="merge",
    )(ys, o, w_ssm_out, w_att_out, g, g, bg, bg)


def _rope_lane_tables(pos):
    half = ROPE_DIM // 2
    inv = 1.0 / (ROPE_THETA ** (jnp.arange(0, ROPE_DIM, 2, dtype=F32) / ROPE_DIM))
    ang = pos.astype(F32)[:, None] * inv[None, :]
    cos, sin = jnp.cos(ang), jnp.sin(ang)
    d = jnp.arange(LANES) % ATT_HEAD_DIM
    lo, hi = d < half, (d >= half) & (d < ROPE_DIM)
    idx = jnp.where(hi, d - half, jnp.where(lo, d, 0))
    c = jnp.where((lo | hi)[None, :], cos[:, idx], 1.0)
    sp = jnp.where(hi[None, :], sin[:, idx], 0.0)
    sm = jnp.where(lo[None, :], -sin[:, idx], 0.0)
    return c, sp, sm


def _split_w_in(w_in, d_model):
    ds = SSM_HEADS * SSM_HEAD_DIM
    cdim = ds + 2 * SSM_GROUPS * SSM_STATE
    dqk = ATT_HEADS * 2 * ATT_HEAD_DIM
    edges = [0, ds, ds + cdim, ds + cdim + SSM_HEADS]
    edges += [edges[-1] + dqk, edges[-1] + 2 * dqk, edges[-1] + 3 * dqk, edges[-1] + 3 * dqk + 2 * d_model]
    parts = [w_in[:, :, a:b].astype(BF16) for a, b in zip(edges[:-1], edges[1:])]
    parts[2] = jnp.pad(parts[2], ((0, 0), (0, 0), (0, LANES - SSM_HEADS)))
    return parts


def _trunk(x, pos, cache_k, cache_v, state_conv, state_ssm, p):
    bsz, t, d = x.shape
    m = bsz * t
    depth = p["w13_ffn1"].shape[0]
    decode = cache_k is not None
    ds = SSM_HEADS * SSM_HEAD_DIM
    cdim = ds + 2 * SSM_GROUPS * SSM_STATE
    dqk = ATT_HEADS * 2 * ATT_HEAD_DIM
    tabs = _rope_lane_tables(pos)
    if decode:
        tabs = tuple(jnp.tile(tb, (bsz, 1)) for tb in tabs)
        period = m
    else:
        period = t
    t_pad = -(-t // SSD_L) * SSD_L
    dt_bias = jnp.pad(p["dt_bias"].astype(F32), ((0, 0), (0, LANES - SSM_HEADS)))

    x = x.reshape(m, d)
    xn = _rmsnorm(x, p["norm_ffn1"][0])
    kstack = vstack = None
    convs, ssms = [], []
    y = None
    for i in range(depth):
        lam_init = 0.8 - 0.6 * math.exp(-0.3 * i)
        h = _swiglu_up(xn, p["w13_ffn1"][i])
        x, xn = _resid_norm(h, p["w2_ffn1"][i], x, p["norm_mix"][i], 0.5)
        (z,) = _proj(xn, p["w_z"][i], (F32,))
        (xbc,) = _proj(xn, p["w_xbc"][i], (F32,))
        (dt,) = _proj(xn, p["w_dt"][i], (F32,), kind="dt", extra=(dt_bias[i:i + 1],))
        (q,) = _proj(xn, p["w_q"][i], (BF16,), kind="rope", extra=tabs, scale=ATT_HEAD_DIM ** -0.5, period=period)
        kstack, k16 = _proj(xn, p["w_k"][i], (F32, BF16), kind="rope", extra=tabs, period=period,
                            stack=(kstack, i, depth), transposed0=not decode)
        vstack, v16 = _proj(xn, p["w_v"][i], (F32, BF16), stack=(vstack, i, depth))
        (g,) = _proj(xn, p["w_g"][i], (F32,))
        xbc3 = xbc.reshape(bsz, t, cdim)
        dt3 = dt.reshape(bsz, t, LANES)
        z3 = z.reshape(bsz, t, ds)
        if t_pad != t:
            padt = ((0, 0), (0, t_pad - t), (0, 0))
            xbc_in, dt_in, z_in = jnp.pad(xbc3, padt), jnp.pad(dt3, padt), jnp.pad(z3, padt)
        else:
            xbc_in, dt_in, z_in = xbc3, dt3, z3
        if decode:
            tail0 = jnp.pad(state_conv[i].astype(F32), ((0, 0), (SUBLANES - (D_CONV - 1), 0), (0, 0)))
            h0 = jnp.transpose(state_ssm[i].astype(F32).reshape(bsz, ds, SSM_STATE), (0, 2, 1))
            prev = state_conv[i].astype(F32)
        else:
            tail0 = h0 = None
            prev = jnp.zeros((bsz, D_CONV - 1, cdim), F32)
        ys, h_t = _ssd(xbc_in, dt_in, z_in, p["conv_w"][i], p["conv_b"][i], p["a_log"][i], p["d_skip"][i],
                       p["ssm_norm"][i], tail0, h0)
        ys = ys[:, :t].reshape(m, ds)
        convs.append(jnp.concatenate([prev, xbc3], axis=1)[:, t:])
        ssms.append(jnp.transpose(h_t, (0, 2, 1)).reshape(bsz, SSM_HEADS, SSM_HEAD_DIM, SSM_STATE))
        lams = tuple(p[n][i].reshape(1, ATT_HEAD_DIM).astype(F32)
                     for n in ("lambda_q1", "lambda_k1", "lambda_q2", "lambda_k2"))
        q3, k3, v3 = (a.reshape(bsz, t, dqk) for a in (q, k16, v16))
        if decode:
            o = _attn_decode(q3, k3, v3, cache_k, cache_v, i, lams, p["subln"][i], lam_init)
        else:
            o = _attn_prompt(q3, k3, v3, lams, p["subln"][i], lam_init)
        mix = _merge(ys, o.reshape(m, dqk), p["w_ssm_out"][i], p["w_att_out"][i], g, p["b_gate"][i])
        x, xn = _resid_norm(mix, p["w_o"][i], x, p["norm_ffn2"][i], 1.0)
        h = _swiglu_up(xn, p["w13_ffn2"][i])
        if i + 1 < depth:
            x, xn = _resid_norm(h, p["w2_ffn2"][i], x, p["norm_ffn1"][i + 1], 0.5)
        else:
            y = _resid_norm(h, p["w2_ffn2"][i], x, p["norm_final"], 0.5, final=True)
    if decode:
        k_all = kstack.reshape(depth, bsz, t, ATT_HEADS, 2, ATT_HEAD_DIM)
    else:
        k_all = jnp.transpose(kstack.reshape(depth, bsz, ATT_HEADS, 2, ATT_HEAD_DIM, t), (0, 1, 5, 2, 3, 4))
    v_all = vstack.reshape(depth, bsz, t, ATT_HEADS, 2 * ATT_HEAD_DIM)
    return y.reshape(bsz, t, d), k_all, v_all, jnp.stack(convs), jnp.stack(ssms)


def kernel(x_prompt, x_sample, cache_k, cache_v, state_conv, state_ssm, norm_ffn1, w13_ffn1, w2_ffn1, norm_mix, w_in, conv_w, conv_b, dt_bias, a_log, d_skip, ssm_norm, w_ssm_out, lambda_q1, lambda_k1, lambda_q2, lambda_k2, subln, w_att_out, b_gate, w_o, norm_ffn2, w13_ffn2, w2_ffn2, norm_final):
    d_model = x_prompt.shape[-1]
    w_z, w_xbc, w_dt, w_q, w_k, w_v, w_g = _split_w_in(w_in, d_model)
    p = dict(
        norm_ffn1=norm_ffn1, w13_ffn1=w13_ffn1.astype(BF16), w2_ffn1=w2_ffn1.astype(BF16), norm_mix=norm_mix,
        w_z=w_z, w_xbc=w_xbc, w_dt=w_dt, w_q=w_q, w_k=w_k, w_v=w_v, w_g=w_g,
        conv_w=conv_w, conv_b=conv_b, dt_bias=dt_bias, a_log=a_log, d_skip=d_skip, ssm_norm=ssm_norm,
        w_ssm_out=w_ssm_out.astype(BF16), lambda_q1=lambda_q1, lambda_k1=lambda_k1, lambda_q2=lambda_q2,
        lambda_k2=lambda_k2, subln=subln, w_att_out=w_att_out.astype(BF16), b_gate=b_gate,
        w_o=w_o.astype(BF16), norm_ffn2=norm_ffn2, w13_ffn2=w13_ffn2.astype(BF16), w2_ffn2=w2_ffn2.astype(BF16),
        norm_final=norm_final)
    depth, dec_b, n_past = cache_k.shape[:3]
    dqk = ATT_HEADS * 2 * ATT_HEAD_DIM
    y_p, k_p, v_p, conv_p, ssm_p = _trunk(x_prompt, jnp.arange(x_prompt.shape[1]), None, None, None, None, p)
    ck = jnp.transpose(cache_k, (0, 1, 3, 4, 5, 2)).reshape(depth, dec_b, dqk, n_past)
    cv = cache_v
    y_s, k_s, v_s, conv_s, ssm_s = _trunk(x_sample, n_past + jnp.arange(x_sample.shape[1]), ck, cv,
                                          state_conv, state_ssm, p)
    return (y_p, y_s, k_p, v_p, conv_p, ssm_p, k_s, v_s, conv_s, ssm_s)
```

```python
import functools
import math

import jax
import jax.numpy as jnp
from jax import lax
from jax.experimental import pallas as pl
from jax.experimental.pallas import tpu as pltpu

F32 = jnp.float32
BF16 = jnp.bfloat16

CHUNK = 64
SSM_HEADS = 32
SSM_HEAD_DIM = 64
SSM_GROUPS = 8
SSM_STATE = 128
D_CONV = 4
ATT_HEADS = 16
ATT_HEAD_DIM = 64
ROPE_DIM = ATT_HEAD_DIM // 4
ROPE_THETA = 500000.0
EPS = 1e-6

LANES = 128
SUBLANES = 8
SSD_L = 128
ATT_TQ = 512
ATT_TK = 256
ATT_HEADS_PER_STEP = 4
ONES_ROWS = 16
VMEM_LIMIT = 56 * 1024 * 1024
NEG = -1e30
LOG2E = math.log2(math.e)


def _cp(sem, vmem=VMEM_LIMIT):
    return pltpu.CompilerParams(dimension_semantics=sem, vmem_limit_bytes=vmem)


def _tile(dim, pref):
    t = min(dim, pref)
    while dim % t:
        t //= 2
    return t


def _sigmoid(x):
    return 1.0 / (1.0 + jnp.exp(-x))


def _nt(a, b):
    return lax.dot_general(a, b, (((1,), (1,)), ((), ())), preferred_element_type=F32)


def _tn(a, b):
    return lax.dot_general(a, b, (((0,), (0,)), ((), ())), preferred_element_type=F32)


def _rmsnorm_kernel(x_ref, g_ref, o_ref):
    x = x_ref[...]
    ms = jnp.mean(x * x, axis=-1, keepdims=True)
    o_ref[...] = (x * lax.rsqrt(ms + EPS) * g_ref[...]).astype(o_ref.dtype)


def _rmsnorm(x, g):
    m, d = x.shape
    tm = _tile(m, 512)
    return pl.pallas_call(
        _rmsnorm_kernel,
        grid=(m // tm,),
        in_specs=[pl.BlockSpec((tm, d), lambda i: (i, 0)), pl.BlockSpec((1, d), lambda i: (0, 0))],
        out_specs=pl.BlockSpec((tm, d), lambda i: (i, 0)),
        out_shape=jax.ShapeDtypeStruct((m, d), BF16),
        compiler_params=_cp(("parallel",)),
        name="rmsnorm",
    )(x, g.reshape(1, d))


def _swiglu_kernel(x_ref, w1_ref, w3_ref, o_ref):
    x = x_ref[...]
    a = jnp.dot(x, w1_ref[...], preferred_element_type=F32)
    b = jnp.dot(x, w3_ref[...], preferred_element_type=F32)
    o_ref[...] = (a * _sigmoid(a) * b).astype(o_ref.dtype)


def _swiglu_up(xn, w13):
    m, d = xn.shape
    f = w13.shape[1] // 2
    tm, tn = _tile(m, 1024), _tile(f, 512)
    nb = f // tn
    return pl.pallas_call(
        _swiglu_kernel,
        grid=(m // tm, nb),
        in_specs=[pl.BlockSpec((tm, d), lambda i, j: (i, 0)),
                  pl.BlockSpec((d, tn), lambda i, j: (0, j)),
                  pl.BlockSpec((d, tn), lambda i, j: (0, j + nb))],
        out_specs=pl.BlockSpec((tm, tn), lambda i, j: (i, j)),
        out_shape=jax.ShapeDtypeStruct((m, f), BF16),
        compiler_params=_cp(("parallel", "arbitrary")),
        name="swiglu_up",
    )(xn, w13, w13)


def _resid_norm_kernel(h_ref, w_ref, x_ref, g_ref, *o_refs, scale, final):
    acc = jnp.dot(h_ref[...], w_ref[...], preferred_element_type=F32)
    xnew = x_ref[...] + scale * acc
    ms = jnp.mean(xnew * xnew, axis=-1, keepdims=True)
    y = xnew * lax.rsqrt(ms + EPS) * g_ref[...]
    if final:
        o_refs[0][...] = y
    else:
        o_refs[0][...] = xnew
        o_refs[1][...] = y.astype(BF16)


def _resid_norm(h, w, x, g, scale, final=False):
    m, k = h.shape
    d = w.shape[1]
    tm = _tile(m, 256)
    row = pl.BlockSpec((tm, d), lambda i: (i, 0))
    if final:
        out_shape, out_specs = jax.ShapeDtypeStruct((m, d), F32), row
    else:
        out_shape = (jax.ShapeDtypeStruct((m, d), F32), jax.ShapeDtypeStruct((m, d), BF16))
        out_specs = (row, row)
    return pl.pallas_call(
        functools.partial(_resid_norm_kernel, scale=scale, final=final),
        grid=(m // tm,),
        in_specs=[pl.BlockSpec((tm, k), lambda i: (i, 0)),
                  pl.BlockSpec((k, d), lambda i: (0, 0), pipeline_mode=pl.Buffered(1)),
                  row,
                  pl.BlockSpec((1, d), lambda i: (0, 0))],
        out_specs=out_specs,
        out_shape=out_shape,
        compiler_params=_cp(("parallel",)),
        name="resid_norm",
    )(h, w, x, g.reshape(1, d))


def _proj_plain_kernel(x_ref, w_ref, *o_refs):
    acc = jnp.dot(x_ref[...], w_ref[...], preferred_element_type=F32)
    for o in o_refs:
        o[...] = acc.astype(o.dtype)


def _proj_dt_kernel(x_ref, w_ref, b_ref, o_ref):
    v = jnp.dot(x_ref[...], w_ref[...], preferred_element_type=F32) + b_ref[...]
    o_ref[...] = jnp.maximum(v, 0.0) + jnp.log1p(jnp.exp(-jnp.abs(v)))


def _proj_rope_kernel(x_ref, w_ref, c_ref, sp_ref, sm_ref, *o_refs, scale, transposed0):
    acc = jnp.dot(x_ref[...], w_ref[...], preferred_element_type=F32)
    c, sp, sm = c_ref[...], sp_ref[...], sm_ref[...]
    half = ROPE_DIM // 2
    for cb in range(acc.shape[1] // LANES):
        sl = slice(cb * LANES, (cb + 1) * LANES)
        y = acc[:, sl]
        r = y * c + pltpu.roll(y, half, 1) * sp + pltpu.roll(y, LANES - half, 1) * sm
        if scale != 1.0:
            r = r * scale
        for n, o in enumerate(o_refs):
            if n == 0 and transposed0:
                o[sl, :] = r.T.astype(o.dtype)
            else:
                o[:, sl] = r.astype(o.dtype)


def _proj(xn, w, out_dtypes, kind="plain", extra=(), scale=1.0, period=None, stack=None, transposed0=False):
    m, d = xn.shape
    n = w.shape[1]
    tm, tn = _tile(m, 1024), _tile(n, 1024)
    if kind == "rope":
        tm = _tile(period, tm)
    in_specs = [pl.BlockSpec((tm, d), lambda i, j: (i, 0)), pl.BlockSpec((d, tn), lambda i, j: (0, j))]
    if kind == "plain":
        body = _proj_plain_kernel
    elif kind == "dt":
        body = _proj_dt_kernel
        in_specs.append(pl.BlockSpec((1, tn), lambda i, j: (0, j)))
    else:
        body = functools.partial(_proj_rope_kernel, scale=scale, transposed0=transposed0)
        nper = period // tm
        in_specs += [pl.BlockSpec((tm, LANES), lambda i, j: (i % nper, 0))] * 3
    out_shape = [jax.ShapeDtypeStruct((m, n), dt) for dt in out_dtypes]
    out_specs = [pl.BlockSpec((tm, tn), lambda i, j: (i, j)) for _ in out_dtypes]
    args = [xn, w, *extra]
    aliases = {}
    if stack is not None:
        buf, layer, depth = stack
        if transposed0:
            out_shape[0] = jax.ShapeDtypeStruct((depth, m // period, n, period), out_dtypes[0])
            out_specs[0] = pl.BlockSpec((None, None, tn, tm), lambda i, j: (layer, i // nper, j, i % nper))
        else:
            out_shape[0] = jax.ShapeDtypeStruct((depth, m, n), out_dtypes[0])
            out_specs[0] = pl.BlockSpec((None, tm, tn), lambda i, j: (layer, i, j))
        if buf is not None:
            n_in = len(args)
            inner = body
            body = lambda *refs: inner(*refs[:n_in], *refs[n_in + 1:])
            in_specs.append(pl.BlockSpec(memory_space=pl.ANY))
            args.append(buf)
            aliases = {n_in: 0}
    return pl.pallas_call(
        body,
        grid=(m // tm, n // tn),
        in_specs=in_specs,
        out_specs=tuple(out_specs),
        out_shape=tuple(out_shape),
        input_output_aliases=aliases,
        compiler_params=_cp(("parallel", "arbitrary")),
        name="proj_" + kind,
    )(*args)


def _conv_silu(read_rows, prev8, cw_ref, cb_ref, sl, rows, width):
    row8 = lax.broadcasted_iota(jnp.int32, (SUBLANES, width), 0)
    x_top = read_rows(0, SUBLANES)
    w_last = cw_ref[D_CONV - 1:D_CONV, sl]
    top = cb_ref[:, sl] + x_top * w_last
    rest = cb_ref[:, sl] + read_rows(SUBLANES, rows) * w_last
    for k in range(1, D_CONV):
        wk = cw_ref[D_CONV - 1 - k:D_CONV - k, sl]
        first = jnp.where(row8 < k, pltpu.roll(prev8, k, 0), pltpu.roll(x_top, k, 0))
        top = top + first * wk
        rest = rest + read_rows(SUBLANES - k, rows - k) * wk
    conv = jnp.concatenate([top, rest], axis=0)
    return conv * _sigmoid(conv)


def _ssd_kernel(xbc_ref, dt_ref, z_ref, alog_ref, dsk_ref, nw_ref, cw_ref, cb_ref, tail0_ref, h0_ref,
                y_ref, hout_ref, h_s, y_s, tail_s, act_s):
    L, N, P = SSD_L, SSM_STATE, SSM_HEAD_DIM
    ds = SSM_HEADS * P
    cdim = ds + 2 * SSM_GROUPS * N
    c = pl.program_id(1)

    @pl.when(c == 0)
    def _():
        tail_s[...] = tail0_ref[0]
        h_s[...] = h0_ref[0]

    cw = 512
    for cb in range(cdim // cw):
        sl = slice(cb * cw, (cb + 1) * cw)
        act_s[:, sl] = _conv_silu(lambda a, b: xbc_ref[0, a:b, sl], tail_s[:, sl], cw_ref, cb_ref, sl, L, cw)
        tail_s[:, sl] = xbc_ref[0, L - SUBLANES:, sl]
    act = lambda sl: act_s[:, sl]

    dtv = dt_ref[0]
    da = dtv * (-jnp.exp(alog_ref[...]) * LOG2E)
    ri = lax.broadcasted_iota(jnp.int32, (L, L), 0)
    ci = lax.broadcasted_iota(jnp.int32, (L, L), 1)
    tri = ri >= ci
    acs = jnp.dot(tri.astype(F32), da, precision=lax.Precision.HIGHEST, preferred_element_type=F32)
    acs_t = acs.T
    acs_end = acs[L - 1:L, :]
    lane = lax.broadcasted_iota(jnp.int32, (L, LANES), 1)
    lane1 = lax.broadcasted_iota(jnp.int32, (1, LANES), 1)
    heads_per_group = SSM_HEADS // SSM_GROUPS
    pairs_per_group = heads_per_group // 2

    for g in range(SSM_GROUPS):
        bg = act(slice(ds + g * N, ds + (g + 1) * N))
        cg = act(slice(ds + (SSM_GROUPS + g) * N, ds + (SSM_GROUPS + g + 1) * N))
        sc = _nt(cg.astype(BF16), bg.astype(BF16))
        for pr in range(pairs_per_group):
            pair = g * pairs_per_group + pr
            psl = slice(pair * LANES, (pair + 1) * LANES)
            xs_pair = act(psl)
            ht_pair = h_s[:, psl]
            ypair = jnp.zeros((L, LANES), F32)
            upd = jnp.zeros((N, LANES), F32)
            dec_lane = jnp.zeros((1, LANES), F32)
            for e in range(2):
                h = pair * 2 + e
                half = (lane >= e * P) & (lane < (e + 1) * P)
                colb = jnp.broadcast_to(acs[:, h:h + 1], (L, LANES))
                dtcolb = jnp.broadcast_to(dtv[:, h:h + 1], (L, LANES))
                rowb = acs_t[h:h + 1, :]
                dec = jnp.exp2(jnp.where(tri, colb - rowb, NEG))
                mh = (sc * dec).astype(BF16)
                ce = (cg * jnp.exp2(colb)).astype(BF16)
                xdt_m = jnp.where(half, xs_pair * dtcolb, 0.0).astype(BF16)
                ht_m = jnp.where(half, ht_pair, 0.0).astype(BF16)
                lhs = jnp.concatenate([mh, ce], axis=1)
                rhs = jnp.concatenate([xdt_m, ht_m], axis=0)
                ypair = ypair + jnp.dot(lhs, rhs, preferred_element_type=F32)
                endb = acs_end[:, h:h + 1]
                bw = (bg * jnp.exp2(endb - colb)).astype(BF16)
                upd = upd + _tn(bw, xdt_m)
                half1 = (lane1 >= e * P) & (lane1 < (e + 1) * P)
                dec_lane = jnp.where(half1, jnp.exp2(endb), dec_lane)
            h_s[:, psl] = ht_pair * dec_lane + upd
            y_s[:, psl] = ypair

    gs = ds // SSM_GROUPS
    for g in range(SSM_GROUPS):
        sl = slice(g * gs, (g + 1) * gs)
        z = z_ref[0, :, sl]
        y = (y_s[:, sl] + dsk_ref[:, sl] * act(sl)) * (z * _sigmoid(z))
        ms = jnp.mean(y * y, axis=-1, keepdims=True)
        y_ref[0, :, sl] = (y * lax.rsqrt(ms + EPS) * nw_ref[:, sl]).astype(y_ref.dtype)

    @pl.when(c == pl.num_programs(1) - 1)
    def _():
        hout_ref[0] = h_s[...]


def _ssd(xbc, dt, z, a_log, d_skip, ssm_norm, conv_w, conv_b, tail0, h0):
    bsz, t, cdim = xbc.shape
    ds = SSM_HEADS * SSM_HEAD_DIM
    L = SSD_L
    assert SSD_L == SSM_STATE == LANES and t % L == 0
    pad = LANES - SSM_HEADS
    alog = jnp.pad(a_log.astype(F32), (0, pad)).reshape(1, LANES)
    dsk = jnp.repeat(d_skip.astype(F32), SSM_HEAD_DIM).reshape(1, ds)
    const = lambda shape: pl.BlockSpec(shape, lambda b, c: (0,) * len(shape))
    in_specs = [pl.BlockSpec((1, L, cdim), lambda b, c: (b, c, 0)),
                pl.BlockSpec((1, L, LANES), lambda b, c: (b, c, 0)),
                pl.BlockSpec((1, L, ds), lambda b, c: (b, c, 0)),
                const((1, LANES)), const((1, ds)), const((1, ds)),
                const((D_CONV, cdim)), const((1, cdim)),
                pl.BlockSpec((1, SUBLANES, cdim), lambda b, c: (b, 0, 0)),
                pl.BlockSpec((1, SSM_STATE, ds), lambda b, c: (b, 0, 0))]
    args = [xbc, dt, z, alog, dsk, ssm_norm.reshape(1, ds), conv_w, conv_b.reshape(1, cdim), tail0, h0]
    scratch = [pltpu.VMEM((SSM_STATE, ds), F32), pltpu.VMEM((L, ds), F32),
               pltpu.VMEM((SUBLANES, cdim), F32), pltpu.VMEM((L, cdim), F32)]
    return pl.pallas_call(
        _ssd_kernel,
        grid=(bsz, t // L),
        in_specs=in_specs,
        out_specs=(pl.BlockSpec((1, L, ds), lambda b, c: (b, c, 0)),
                   pl.BlockSpec((1, SSM_STATE, ds), lambda b, c: (b, 0, 0))),
        out_shape=(jax.ShapeDtypeStruct((bsz, t, ds), BF16),
                   jax.ShapeDtypeStruct((bsz, SSM_STATE, ds), F32)),
        scratch_shapes=scratch,
        compiler_params=_cp(("parallel", "arbitrary")),
        name="ssd",
    )(*args)


def _lam(lq1, lk1, lq2, lk2, lam_init):
    return (jnp.exp(jnp.sum(lq1[...] * lk1[...], axis=-1, keepdims=True))
            - jnp.exp(jnp.sum(lq2[...] * lk2[...], axis=-1, keepdims=True)) + lam_init)


def _split_q(q):
    lane = lax.broadcasted_iota(jnp.int32, q.shape, 1)
    zero = jnp.zeros_like(q)
    return jnp.where(lane < ATT_HEAD_DIM, q, zero), jnp.where(lane >= ATT_HEAD_DIM, q, zero)


def _subln(o, sub_ref, lam_init):
    ms = jnp.mean(o * o, axis=-1, keepdims=True)
    return o * lax.rsqrt(ms + EPS) * sub_ref[...] * (1.0 - lam_init)


def _attn_prompt_kernel(lq1, lk1, lq2, lk2, sub_ref, q_ref, k_ref, v_ref, o_ref,
                        vt_s, sa_s, sb_s, m_s, acc_s, *, tq, tk, heads, lam_init):
    qi = pl.program_id(2)
    hw = 2 * ATT_HEAD_DIM
    t = k_ref.shape[1]

    @pl.when(qi == 0)
    def _():
        for g in range(heads):
            for cb in range(t // tk):
                rows = slice(cb * tk, (cb + 1) * tk)
                vt_s[g, :hw, rows] = v_ref[0, rows, g * hw:(g + 1) * hw].astype(F32).T.astype(BF16)
            vt_s[g, hw:, :] = jnp.ones((ONES_ROWS, t), BF16)

    start = pl.multiple_of(qi * tq, tq)
    kr = lax.broadcasted_iota(jnp.int32, (tk, 2 * tq), 0)
    qc = lax.broadcasted_iota(jnp.int32, (tk, 2 * tq), 1)
    qc = jnp.where(qc >= tq, qc - tq, qc)
    qq = []
    for g in range(heads):
        q1, q2 = _split_q(q_ref[0, :, g * hw:(g + 1) * hw])
        qq.append(jnp.concatenate([q1, q2], axis=0))

    def scores(off, s_ref):
        for g in range(heads):
            s_ref[g] = _nt(k_ref[0, pl.ds(off, tk), g * hw:(g + 1) * hw], qq[g])

    def consume(off, s_ref, diag):
        for g in range(heads):
            s = s_ref[g]
            if diag is not None:
                s = jnp.where(((kr + diag * tk) // CHUNK) <= (qc // CHUNK), s, NEG)
            m = m_s[g]
            mn = jnp.maximum(m, jnp.max(s, axis=0, keepdims=True))
            alpha = jnp.exp(m - mn)
            p = jnp.exp(s - mn)
            pv = jnp.dot(vt_s[g, :, pl.ds(off, tk)], p.astype(BF16), preferred_element_type=F32)
            m_s[g] = mn
            acc_s[g] = alpha * acc_s[g] + pv

    m_s[...] = jnp.full(m_s.shape, NEG, F32)
    acc_s[...] = jnp.zeros(acc_s.shape, F32)
    scores(0, sa_s)

    def pair(jp, carry):
        off = pl.multiple_of(2 * jp * tk, tq)
        scores(off + tk, sb_s)
        consume(off, sa_s, None)
        scores(off + 2 * tk, sa_s)
        consume(off + tk, sb_s, None)
        return carry

    lax.fori_loop(0, qi, pair, 0)
    scores(start + tk, sb_s)
    consume(start, sa_s, 0)
    consume(start + tk, sb_s, 1)

    lam = _lam(lq1, lk1, lq2, lk2, lam_init)
    for g in range(heads):
        an = acc_s[g, :hw, :] / acc_s[g, hw:hw + 1, :]
        o_t = an[:, :tq] - lam * an[:, tq:]
        ms = jnp.mean(o_t * o_t, axis=0, keepdims=True)
        o_t = o_t * lax.rsqrt(ms + EPS) * sub_ref[...] * (1.0 - lam_init)
        o_ref[0, :, g * hw:(g + 1) * hw] = o_t.T.astype(o_ref.dtype)


def _attn_prompt(q, k, v, lams, subln, lam_init):
    bsz, t, dq = q.shape
    tq, tk = ATT_TQ, ATT_TK
    assert t % tq == 0 and tq == 2 * tk and tk % CHUNK == 0
    hw = 2 * ATT_HEAD_DIM
    heads = ATT_HEADS_PER_STEP
    gw = heads * hw
    vec = pl.BlockSpec((1, ATT_HEAD_DIM), lambda b, h, i: (0, 0))
    return pl.pallas_call(
        functools.partial(_attn_prompt_kernel, tq=tq, tk=tk, heads=heads, lam_init=lam_init),
        grid=(bsz, dq // gw, t // tq),
        in_specs=[vec, vec, vec, vec,
                  pl.BlockSpec((hw, 1), lambda b, h, i: (0, 0)),
                  pl.BlockSpec((1, tq, gw), lambda b, h, i: (b, i, h)),
                  pl.BlockSpec((1, t, gw), lambda b, h, i: (b, 0, h)),
                  pl.BlockSpec((1, t, gw), lambda b, h, i: (b, 0, h))],
        out_specs=pl.BlockSpec((1, tq, gw), lambda b, h, i: (b, i, h)),
        out_shape=jax.ShapeDtypeStruct((bsz, t, dq), BF16),
        scratch_shapes=[pltpu.VMEM((heads, hw + ONES_ROWS, t), BF16),
                        pltpu.VMEM((heads, tk, 2 * tq), F32), pltpu.VMEM((heads, tk, 2 * tq), F32),
                        pltpu.VMEM((heads, 1, 2 * tq), F32),
                        pltpu.VMEM((heads, hw + ONES_ROWS, 2 * tq), F32)],
        compiler_params=_cp(("parallel", "parallel", "arbitrary")),
        name="attn_prompt",
    )(*lams, subln.reshape(hw, 1), q, k, v)


def _attn_decode_kernel(lq1, lk1, lq2, lk2, sub_ref, q_ref, kc_ref, vc_ref, kn_ref, vn_ref, o_ref, *, lam_init):
    lam = _lam(lq1, lk1, lq2, lk2, lam_init)
    hw = 2 * ATT_HEAD_DIM
    nheads = q_ref.shape[2] // hw
    npast = vc_ref.shape[2] // nheads
    for h in range(nheads):
        sl = slice(h * hw, (h + 1) * hw)
        q1, q2 = _split_q(q_ref[0, :, sl])
        qq = jnp.concatenate([q1, q2], axis=0)
        t = q1.shape[0]
        kct = kc_ref[0, 0, sl, :].astype(BF16)
        vc = vc_ref[0, 0, pl.ds(h, npast, stride=nheads), :].astype(BF16)
        s_c = jnp.dot(qq, kct, preferred_element_type=F32)
        s_n = _nt(qq, kn_ref[0, :, sl])
        m = jnp.maximum(jnp.max(s_c, axis=-1, keepdims=True), jnp.max(s_n, axis=-1, keepdims=True))
        p_c, p_n = jnp.exp(s_c - m), jnp.exp(s_n - m)
        l = jnp.sum(p_c, axis=-1, keepdims=True) + jnp.sum(p_n, axis=-1, keepdims=True)
        p_c, p_n = p_c / l, p_n / l
        o = (jnp.dot((p_c[:t] - lam * p_c[t:]).astype(BF16), vc, preferred_element_type=F32)
             + jnp.dot((p_n[:t] - lam * p_n[t:]).astype(BF16), vn_ref[0, :, sl], preferred_element_type=F32))
        o_ref[0, :, sl] = _subln(o, sub_ref, lam_init).astype(o_ref.dtype)


def _attn_decode(q, kn, vn, cache_k, cache_v, layer, lams, subln, lam_init):
    bsz, t, dq = q.shape
    npast = cache_v.shape[2] // (dq // (2 * ATT_HEAD_DIM))
    assert (npast + t - 1) // CHUNK <= npast // CHUNK, "decode kernel assumes every key is visible"
    hw = 2 * ATT_HEAD_DIM
    vec = pl.BlockSpec((1, ATT_HEAD_DIM), lambda b: (0, 0))
    tok = pl.BlockSpec((1, t, dq), lambda b: (b, 0, 0))
    cache = pl.BlockSpec((1, 1, npast * (dq // hw), hw), lambda b: (layer, b, 0, 0))
    cache_t = pl.BlockSpec((1, 1, dq, npast), lambda b: (layer, b, 0, 0))
    return pl.pallas_call(
        functools.partial(_attn_decode_kernel, lam_init=lam_init),
        grid=(bsz,),
        in_specs=[vec, vec, vec, vec, pl.BlockSpec((1, hw), lambda b: (0, 0)), tok, cache_t, cache, tok, tok],
        out_specs=tok,
        out_shape=jax.ShapeDtypeStruct((bsz, t, dq), BF16),
        compiler_params=_cp(("parallel",)),
        name="attn_decode",
    )(*lams, subln.reshape(1, hw), q, cache_k, cache_v, kn, vn)


def _merge_kernel(ys_ref, o_ref, wa_ref, wb_ref, g1_ref, g2_ref, b1_ref, b2_ref, out_ref):
    a = jnp.dot(ys_ref[...], wa_ref[...], preferred_element_type=F32)
    b = jnp.dot(o_ref[...], wb_ref[...], preferred_element_type=F32)
    g1 = _sigmoid(g1_ref[...] + b1_ref[...])
    g2 = _sigmoid(g2_ref[...] + b2_ref[...])
    out_ref[...] = (g1 * a + g2 * b).astype(out_ref.dtype)


def _merge(ys, o, w_ssm_out, w_att_out, g, b_gate):
    m, ka = ys.shape
    kb = o.shape[1]
    d = w_ssm_out.shape[1]
    tm, tn = _tile(m, 1024), _tile(d, 512)
    nb = d // tn
    bg = b_gate.reshape(1, 2 * d)
    return pl.pallas_call(
        _merge_kernel,
        grid=(m // tm, nb),
        in_specs=[pl.BlockSpec((tm, ka), lambda i, j: (i, 0)),
                  pl.BlockSpec((tm, kb), lambda i, j: (i, 0)),
                  pl.BlockSpec((ka, tn), lambda i, j: (0, j)),
                  pl.BlockSpec((kb, tn), lambda i, j: (0, j)),
                  pl.BlockSpec((tm, tn), lambda i, j: (i, j)),
                  pl.BlockSpec((tm, tn), lambda i, j: (i, j + nb)),
                  pl.BlockSpec((1, tn), lambda i, j: (0, j)),
                  pl.BlockSpec((1, tn), lambda i, j: (0, j + nb))],
        out_specs=pl.BlockSpec((tm, tn), lambda i, j: (i, j)),
        out_shape=jax.ShapeDtypeStruct((m, d), BF16),
        compiler_params=_cp(("parallel", "arbitrary")),
        name="merge",
    )(ys, o, w_ssm_out, w_att_out, g, g, bg, bg)


def _rope_lane_tables(pos):
    half = ROPE_DIM // 2
    inv = 1.0 / (ROPE_THETA ** (jnp.arange(0, ROPE_DIM, 2, dtype=F32) / ROPE_DIM))
    ang = pos.astype(F32)[:, None] * inv[None, :]
    cos, sin = jnp.cos(ang), jnp.sin(ang)
    d = jnp.arange(LANES) % ATT_HEAD_DIM
    lo, hi = d < half, (d >= half) & (d < ROPE_DIM)
    idx = jnp.where(hi, d - half, jnp.where(lo, d, 0))
    c = jnp.where((lo | hi)[None, :], cos[:, idx], 1.0)
    sp = jnp.where(hi[None, :], sin[:, idx], 0.0)
    sm = jnp.where(lo[None, :], -sin[:, idx], 0.0)
    return c, sp, sm


def _split_w_in(w_in, d_model):
    ds = SSM_HEADS * SSM_HEAD_DIM
    cdim = ds + 2 * SSM_GROUPS * SSM_STATE
    dqk = ATT_HEADS * 2 * ATT_HEAD_DIM
    edges = [0, ds, ds + cdim, ds + cdim + SSM_HEADS]
    edges += [edges[-1] + dqk, edges[-1] + 2 * dqk, edges[-1] + 3 * dqk, edges[-1] + 3 * dqk + 2 * d_model]
    parts = [w_in[:, :, a:b].astype(BF16) for a, b in zip(edges[:-1], edges[1:])]
    parts[2] = jnp.pad(parts[2], ((0, 0), (0, 0), (0, LANES - SSM_HEADS)))
    return parts


def _trunk(x, pos, cache_k, cache_v, state_conv, state_ssm, p):
    bsz, t, d = x.shape
    m = bsz * t
    depth = p["w13_ffn1"].shape[0]
    decode = cache_k is not None
    ds = SSM_HEADS * SSM_HEAD_DIM
    cdim = ds + 2 * SSM_GROUPS * SSM_STATE
    dqk = ATT_HEADS * 2 * ATT_HEAD_DIM
    tabs = _rope_lane_tables(pos)
    if decode:
        tabs = tuple(jnp.tile(tb, (bsz, 1)) for tb in tabs)
        period = m
    else:
        period = t
    t_pad = -(-t // SSD_L) * SSD_L
    dt_bias = jnp.pad(p["dt_bias"].astype(F32), ((0, 0), (0, LANES - SSM_HEADS)))

    x = x.reshape(m, d)
    xn = _rmsnorm(x, p["norm_ffn1"][0])
    kstack = vstack = None
    convs, ssms = [], []
    y = None
    for i in range(depth):
        lam_init = 0.8 - 0.6 * math.exp(-0.3 * i)
        h = _swiglu_up(xn, p["w13_ffn1"][i])
        x, xn = _resid_norm(h, p["w2_ffn1"][i], x, p["norm_mix"][i], 0.5)
        (z,) = _proj(xn, p["w_z"][i], (F32,))
        (xbc,) = _proj(xn, p["w_xbc"][i], (F32,))
        (dt,) = _proj(xn, p["w_dt"][i], (F32,), kind="dt", extra=(dt_bias[i:i + 1],))
        (q,) = _proj(xn, p["w_q"][i], (BF16,), kind="rope", extra=tabs, scale=ATT_HEAD_DIM ** -0.5, period=period)
        kstack, k16 = _proj(xn, p["w_k"][i], (F32, BF16), kind="rope", extra=tabs, period=period,
                            stack=(kstack, i, depth), transposed0=not decode)
        vstack, v16 = _proj(xn, p["w_v"][i], (F32, BF16), stack=(vstack, i, depth))
        (g,) = _proj(xn, p["w_g"][i], (F32,))
        xbc3 = xbc.reshape(bsz, t, cdim)
        dt3 = dt.reshape(bsz, t, LANES)
        z3 = z.reshape(bsz, t, ds)
        if t_pad != t:
            padt = ((0, 0), (0, t_pad - t), (0, 0))
            xbc_in, dt_in, z_in = jnp.pad(xbc3, padt), jnp.pad(dt3, padt), jnp.pad(z3, padt)
        else:
            xbc_in, dt_in, z_in = xbc3, dt3, z3
        if decode:
            prev = state_conv[i].astype(F32)
            h0 = jnp.transpose(state_ssm[i].astype(F32).reshape(bsz, ds, SSM_STATE), (0, 2, 1))
        else:
            prev = jnp.zeros((bsz, D_CONV - 1, cdim), F32)
            h0 = jnp.zeros((bsz, SSM_STATE, ds), F32)
        tail0 = jnp.pad(prev, ((0, 0), (SUBLANES - (D_CONV - 1), 0), (0, 0)))
        ys, h_t = _ssd(xbc_in, dt_in, z_in, p["a_log"][i], p["d_skip"][i], p["ssm_norm"][i],
                       p["conv_w"][i], p["conv_b"][i], tail0, h0)
        ys = ys[:, :t].reshape(m, ds)
        convs.append(jnp.concatenate([prev, xbc3], axis=1)[:, t:])
        ssms.append(jnp.transpose(h_t, (0, 2, 1)).reshape(bsz, SSM_HEADS, SSM_HEAD_DIM, SSM_STATE))
        lams = tuple(p[n][i].reshape(1, ATT_HEAD_DIM).astype(F32)
                     for n in ("lambda_q1", "lambda_k1", "lambda_q2", "lambda_k2"))
        q3, k3, v3 = (a.reshape(bsz, t, dqk) for a in (q, k16, v16))
        if decode:
            o = _attn_decode(q3, k3, v3, cache_k, cache_v, i, lams, p["subln"][i], lam_init)
        else:
            o = _attn_prompt(q3, k3, v3, lams, p["subln"][i], lam_init)
        mix = _merge(ys, o.reshape(m, dqk), p["w_ssm_out"][i], p["w_att_out"][i], g, p["b_gate"][i])
        x, xn = _resid_norm(mix, p["w_o"][i], x, p["norm_ffn2"][i], 1.0)
        h = _swiglu_up(xn, p["w13_ffn2"][i])
        if i + 1 < depth:
            x, xn = _resid_norm(h, p["w2_ffn2"][i], x, p["norm_ffn1"][i + 1], 0.5)
        else:
            y = _resid_norm(h, p["w2_ffn2"][i], x, p["norm_final"], 0.5, final=True)
    if decode:
        k_all = kstack.reshape(depth, bsz, t, ATT_HEADS, 2, ATT_HEAD_DIM)
    else:
        k_all = jnp.transpose(kstack.reshape(depth, bsz, ATT_HEADS, 2, ATT_HEAD_DIM, t), (0, 1, 5, 2, 3, 4))
    v_all = vstack.reshape(depth, bsz, t, ATT_HEADS, 2 * ATT_HEAD_DIM)
    return y.reshape(bsz, t, d), k_all, v_all, jnp.stack(convs), jnp.stack(ssms)


def kernel(x_prompt, x_sample, cache_k, cache_v, state_conv, state_ssm, norm_ffn1, w13_ffn1, w2_ffn1, norm_mix, w_in, conv_w, conv_b, dt_bias, a_log, d_skip, ssm_norm, w_ssm_out, lambda_q1, lambda_k1, lambda_q2, lambda_k2, subln, w_att_out, b_gate, w_o, norm_ffn2, w13_ffn2, w2_ffn2, norm_final):
    d_model = x_prompt.shape[-1]
    w_z, w_xbc, w_dt, w_q, w_k, w_v, w_g = _split_w_in(w_in, d_model)
    p = dict(
        norm_ffn1=norm_ffn1, w13_ffn1=w13_ffn1.astype(BF16), w2_ffn1=w2_ffn1.astype(BF16), norm_mix=norm_mix,
        w_z=w_z, w_xbc=w_xbc, w_dt=w_dt, w_q=w_q, w_k=w_k, w_v=w_v, w_g=w_g,
        conv_w=conv_w, conv_b=conv_b, dt_bias=dt_bias, a_log=a_log, d_skip=d_skip, ssm_norm=ssm_norm,
        w_ssm_out=w_ssm_out.astype(BF16), lambda_q1=lambda_q1, lambda_k1=lambda_k1, lambda_q2=lambda_q2,
        lambda_k2=lambda_k2, subln=subln, w_att_out=w_att_out.astype(BF16), b_gate=b_gate,
        w_o=w_o.astype(BF16), norm_ffn2=norm_ffn2, w13_ffn2=w13_ffn2.astype(BF16), w2_ffn2=w2_ffn2.astype(BF16),
        norm_final=norm_final)
    depth, dec_b, n_past = cache_k.shape[:3]
    dqk = ATT_HEADS * 2 * ATT_HEAD_DIM
    y_p, k_p, v_p, conv_p, ssm_p = _trunk(x_prompt, jnp.arange(x_prompt.shape[1]), None, None, None, None, p)
    ck = jnp.transpose(cache_k, (0, 1, 3, 4, 5, 2)).reshape(depth, dec_b, dqk, n_past)
    cv = cache_v.reshape(depth, dec_b, n_past * ATT_HEADS, 2 * ATT_HEAD_DIM)
    y_s, k_s, v_s, conv_s, ssm_s = _trunk(x_sample, n_past + jnp.arange(x_sample.shape[1]), ck, cv,
                                          state_conv, state_ssm, p)
    return (y_p, y_s, k_p, v_p, conv_p, ssm_p, k_s, v_s, conv_s, ssm_s)
```

```python
import functools
import math

import jax
import jax.numpy as jnp
from jax import lax
from jax.experimental import pallas as pl
from jax.experimental.pallas import tpu as pltpu

F32 = jnp.float32
BF16 = jnp.bfloat16

CHUNK = 64
SSM_HEADS = 32
SSM_HEAD_DIM = 64
SSM_GROUPS = 8
SSM_STATE = 128
D_CONV = 4
ATT_HEADS = 16
ATT_HEAD_DIM = 64
ROPE_DIM = ATT_HEAD_DIM // 4
ROPE_THETA = 500000.0
EPS = 1e-6

LANES = 128
SUBLANES = 8
SSD_L = 128
ATT_TQ = 512
ATT_TK = 256
ATT_HEADS_PER_STEP = 4
ONES_ROWS = 16
VMEM_LIMIT = 56 * 1024 * 1024
NEG = -1e30
LOG2E = math.log2(math.e)


def _cp(sem, vmem=VMEM_LIMIT):
    return pltpu.CompilerParams(dimension_semantics=sem, vmem_limit_bytes=vmem)


def _tile(dim, pref):
    t = min(dim, pref)
    while dim % t:
        t //= 2
    return t


def _sigmoid(x):
    return 1.0 / (1.0 + jnp.exp(-x))


def _silu(x):
    h = 0.5 * x
    return h + h * jnp.tanh(h)


def _nt(a, b):
    return lax.dot_general(a, b, (((1,), (1,)), ((), ())), preferred_element_type=F32)


def _tn(a, b):
    return lax.dot_general(a, b, (((0,), (0,)), ((), ())), preferred_element_type=F32)


def _rmsnorm_kernel(x_ref, g_ref, o_ref):
    x = x_ref[...]
    ms = jnp.mean(x * x, axis=-1, keepdims=True)
    o_ref[...] = (x * lax.rsqrt(ms + EPS) * g_ref[...]).astype(o_ref.dtype)


def _rmsnorm(x, g):
    m, d = x.shape
    tm = _tile(m, 512)
    return pl.pallas_call(
        _rmsnorm_kernel,
        grid=(m // tm,),
        in_specs=[pl.BlockSpec((tm, d), lambda i: (i, 0)), pl.BlockSpec((1, d), lambda i: (0, 0))],
        out_specs=pl.BlockSpec((tm, d), lambda i: (i, 0)),
        out_shape=jax.ShapeDtypeStruct((m, d), BF16),
        compiler_params=_cp(("parallel",)),
        name="rmsnorm",
    )(x, g.reshape(1, d))


def _swiglu_kernel(x_ref, w1_ref, w3_ref, o_ref):
    x = x_ref[...]
    a = jnp.dot(x, w1_ref[...], preferred_element_type=F32)
    b = jnp.dot(x, w3_ref[...], preferred_element_type=F32)
    o_ref[...] = (a * _sigmoid(a) * b).astype(o_ref.dtype)


def _swiglu_up(xn, w13):
    m, d = xn.shape
    f = w13.shape[1] // 2
    tm, tn = _tile(m, 1024), _tile(f, 512)
    nb = f // tn
    return pl.pallas_call(
        _swiglu_kernel,
        grid=(m // tm, nb),
        in_specs=[pl.BlockSpec((tm, d), lambda i, j: (i, 0)),
                  pl.BlockSpec((d, tn), lambda i, j: (0, j)),
                  pl.BlockSpec((d, tn), lambda i, j: (0, j + nb))],
        out_specs=pl.BlockSpec((tm, tn), lambda i, j: (i, j)),
        out_shape=jax.ShapeDtypeStruct((m, f), BF16),
        compiler_params=_cp(("parallel", "arbitrary")),
        name="swiglu_up",
    )(xn, w13, w13)


def _resid_norm_kernel(h_ref, w_ref, x_ref, g_ref, *o_refs, scale, final):
    acc = jnp.dot(h_ref[...], w_ref[...], preferred_element_type=F32)
    xnew = x_ref[...] + scale * acc
    ms = jnp.mean(xnew * xnew, axis=-1, keepdims=True)
    y = xnew * lax.rsqrt(ms + EPS) * g_ref[...]
    if final:
        o_refs[0][...] = y
    else:
        o_refs[0][...] = xnew
        o_refs[1][...] = y.astype(BF16)


def _resid_norm(h, w, x, g, scale, final=False):
    m, k = h.shape
    d = w.shape[1]
    tm = _tile(m, 256)
    row = pl.BlockSpec((tm, d), lambda i: (i, 0))
    if final:
        out_shape, out_specs = jax.ShapeDtypeStruct((m, d), F32), row
    else:
        out_shape = (jax.ShapeDtypeStruct((m, d), F32), jax.ShapeDtypeStruct((m, d), BF16))
        out_specs = (row, row)
    return pl.pallas_call(
        functools.partial(_resid_norm_kernel, scale=scale, final=final),
        grid=(m // tm,),
        in_specs=[pl.BlockSpec((tm, k), lambda i: (i, 0)),
                  pl.BlockSpec((k, d), lambda i: (0, 0), pipeline_mode=pl.Buffered(1)),
                  row,
                  pl.BlockSpec((1, d), lambda i: (0, 0))],
        out_specs=out_specs,
        out_shape=out_shape,
        compiler_params=_cp(("parallel",)),
        name="resid_norm",
    )(h, w, x, g.reshape(1, d))


def _proj_plain_kernel(x_ref, w_ref, *o_refs):
    acc = jnp.dot(x_ref[...], w_ref[...], preferred_element_type=F32)
    for o in o_refs:
        o[...] = acc.astype(o.dtype)


def _proj_dt_kernel(x_ref, w_ref, b_ref, o_ref):
    v = jnp.dot(x_ref[...], w_ref[...], preferred_element_type=F32) + b_ref[...]
    o_ref[...] = jnp.maximum(v, 0.0) + jnp.log1p(jnp.exp(-jnp.abs(v)))


def _proj_rope_kernel(x_ref, w_ref, c_ref, sp_ref, sm_ref, *o_refs, scale, transposed0):
    acc = jnp.dot(x_ref[...], w_ref[...], preferred_element_type=F32)
    c, sp, sm = c_ref[...], sp_ref[...], sm_ref[...]
    half = ROPE_DIM // 2
    for cb in range(acc.shape[1] // LANES):
        sl = slice(cb * LANES, (cb + 1) * LANES)
        y = acc[:, sl]
        r = y * c + pltpu.roll(y, half, 1) * sp + pltpu.roll(y, LANES - half, 1) * sm
        if scale != 1.0:
            r = r * scale
        for n, o in enumerate(o_refs):
            if n == 0 and transposed0:
                o[sl, :] = r.T.astype(o.dtype)
            else:
                o[:, sl] = r.astype(o.dtype)


def _proj(xn, w, out_dtypes, kind="plain", extra=(), scale=1.0, period=None, stack=None, transposed0=False):
    m, d = xn.shape
    n = w.shape[1]
    tm, tn = _tile(m, 1024), _tile(n, 1024)
    if kind == "rope":
        tm = _tile(period, tm)
    in_specs = [pl.BlockSpec((tm, d), lambda i, j: (i, 0)), pl.BlockSpec((d, tn), lambda i, j: (0, j))]
    if kind == "plain":
        body = _proj_plain_kernel
    elif kind == "dt":
        body = _proj_dt_kernel
        in_specs.append(pl.BlockSpec((1, tn), lambda i, j: (0, j)))
    else:
        body = functools.partial(_proj_rope_kernel, scale=scale, transposed0=transposed0)
        nper = period // tm
        in_specs += [pl.BlockSpec((tm, LANES), lambda i, j: (i % nper, 0))] * 3
    out_shape = [jax.ShapeDtypeStruct((m, n), dt) for dt in out_dtypes]
    out_specs = [pl.BlockSpec((tm, tn), lambda i, j: (i, j)) for _ in out_dtypes]
    args = [xn, w, *extra]
    aliases = {}
    if stack is not None:
        buf, layer, depth = stack
        if transposed0:
            out_shape[0] = jax.ShapeDtypeStruct((depth, m // period, n, period), out_dtypes[0])
            out_specs[0] = pl.BlockSpec((None, None, tn, tm), lambda i, j: (layer, i // nper, j, i % nper))
        else:
            out_shape[0] = jax.ShapeDtypeStruct((depth, m, n), out_dtypes[0])
            out_specs[0] = pl.BlockSpec((None, tm, tn), lambda i, j: (layer, i, j))
        if buf is not None:
            n_in = len(args)
            inner = body
            body = lambda *refs: inner(*refs[:n_in], *refs[n_in + 1:])
            in_specs.append(pl.BlockSpec(memory_space=pl.ANY))
            args.append(buf)
            aliases = {n_in: 0}
    return pl.pallas_call(
        body,
        grid=(m // tm, n // tn),
        in_specs=in_specs,
        out_specs=tuple(out_specs),
        out_shape=tuple(out_shape),
        input_output_aliases=aliases,
        compiler_params=_cp(("parallel", "arbitrary")),
        name="proj_" + kind,
    )(*args)


def _conv_silu(read_rows, prev8, cw_ref, cb_ref, sl, rows, width):
    row8 = lax.broadcasted_iota(jnp.int32, (SUBLANES, width), 0)
    x_top = read_rows(0, SUBLANES)
    w_last = cw_ref[D_CONV - 1:D_CONV, sl]
    top = cb_ref[:, sl] + x_top * w_last
    rest = cb_ref[:, sl] + read_rows(SUBLANES, rows) * w_last
    for k in range(1, D_CONV):
        wk = cw_ref[D_CONV - 1 - k:D_CONV - k, sl]
        first = jnp.where(row8 < k, pltpu.roll(prev8, k, 0), pltpu.roll(x_top, k, 0))
        top = top + first * wk
        rest = rest + read_rows(SUBLANES - k, rows - k) * wk
    conv = jnp.concatenate([top, rest], axis=0)
    return _silu(conv)


def _ssd_kernel(xbc_ref, dt_ref, z_ref, alog_ref, dsk_ref, nw_ref, cw_ref, cb_ref, tail0_ref, h0_ref,
                y_ref, hout_ref, h_s, y_s, tail_s, act_s):
    L, N, P = SSD_L, SSM_STATE, SSM_HEAD_DIM
    ds = SSM_HEADS * P
    cdim = ds + 2 * SSM_GROUPS * N
    c = pl.program_id(1)

    @pl.when(c == 0)
    def _():
        tail_s[...] = tail0_ref[0]
        h_s[...] = h0_ref[0]

    cw = 512
    for cb in range(cdim // cw):
        sl = slice(cb * cw, (cb + 1) * cw)
        act_s[:, sl] = _conv_silu(lambda a, b: xbc_ref[0, a:b, sl], tail_s[:, sl], cw_ref, cb_ref, sl, L, cw)
        tail_s[:, sl] = xbc_ref[0, L - SUBLANES:, sl]
    act = lambda sl: act_s[:, sl]

    dtv = dt_ref[0]
    da = dtv * (-jnp.exp(alog_ref[...]) * LOG2E)
    ri = lax.broadcasted_iota(jnp.int32, (L, L), 0)
    ci = lax.broadcasted_iota(jnp.int32, (L, L), 1)
    tri = ri >= ci
    acs = jnp.dot(tri.astype(F32), da, precision=lax.Precision.HIGHEST, preferred_element_type=F32)
    acs_t = acs.T
    acs_end = acs[L - 1:L, :]
    lane = lax.broadcasted_iota(jnp.int32, (L, LANES), 1)
    lane1 = lax.broadcasted_iota(jnp.int32, (1, LANES), 1)
    heads_per_group = SSM_HEADS // SSM_GROUPS
    pairs_per_group = heads_per_group // 2

    for g in range(SSM_GROUPS):
        bg = act(slice(ds + g * N, ds + (g + 1) * N))
        cg = act(slice(ds + (SSM_GROUPS + g) * N, ds + (SSM_GROUPS + g + 1) * N))
        sc = _nt(cg.astype(BF16), bg.astype(BF16))
        heads = [g * heads_per_group + r for r in range(heads_per_group)]
        colbs = [jnp.broadcast_to(acs[:, h:h + 1], (L, LANES)) for h in heads]
        dtcolbs = [jnp.broadcast_to(dtv[:, h:h + 1], (L, LANES)) for h in heads]
        lhss, bws, xdts, hts = [], [], [], []
        for r, h in enumerate(heads):
            e = h % 2
            psl = slice((h // 2) * LANES, (h // 2 + 1) * LANES)
            half = (lane >= e * P) & (lane < (e + 1) * P)
            colb = colbs[r]
            dec = jnp.exp2(jnp.where(tri, colb - acs_t[h:h + 1, :], NEG))
            mh = (sc * dec).astype(BF16)
            ce = (cg * jnp.exp2(colb)).astype(BF16)
            lhss.append(jnp.concatenate([mh, ce], axis=1))
            xdts.append(jnp.where(half, act(psl) * dtcolbs[r], 0.0).astype(BF16))
            hts.append(jnp.where(half, h_s[:, psl], 0.0).astype(BF16))
            bws.append((bg * jnp.exp2(acs_end[:, h:h + 1] - colb)).astype(BF16))
        for pr in range(pairs_per_group):
            pair = g * pairs_per_group + pr
            psl = slice(pair * LANES, (pair + 1) * LANES)
            ypair = jnp.zeros((L, LANES), F32)
            upd = jnp.zeros((N, LANES), F32)
            dec_lane = jnp.zeros((1, LANES), F32)
            for e in range(2):
                r = pr * 2 + e
                h = heads[r]
                rhs = jnp.concatenate([xdts[r], hts[r]], axis=0)
                ypair = ypair + jnp.dot(lhss[r], rhs, preferred_element_type=F32)
                upd = upd + _tn(bws[r], xdts[r])
                half1 = (lane1 >= e * P) & (lane1 < (e + 1) * P)
                dec_lane = jnp.where(half1, jnp.exp2(acs_end[:, h:h + 1]), dec_lane)
            h_s[:, psl] = h_s[:, psl] * dec_lane + upd
            y_s[:, psl] = ypair

    gs = ds // SSM_GROUPS
    for g in range(SSM_GROUPS):
        sl = slice(g * gs, (g + 1) * gs)
        z = z_ref[0, :, sl]
        y = (y_s[:, sl] + dsk_ref[:, sl] * act(sl)) * _silu(z)
        ms = jnp.mean(y * y, axis=-1, keepdims=True)
        y_ref[0, :, sl] = (y * lax.rsqrt(ms + EPS) * nw_ref[:, sl]).astype(y_ref.dtype)

    @pl.when(c == pl.num_programs(1) - 1)
    def _():
        hout_ref[0] = h_s[...]


def _ssd(xbc, dt, z, a_log, d_skip, ssm_norm, conv_w, conv_b, tail0, h0):
    bsz, t, cdim = xbc.shape
    ds = SSM_HEADS * SSM_HEAD_DIM
    L = SSD_L
    assert SSD_L == SSM_STATE == LANES and t % L == 0
    pad = LANES - SSM_HEADS
    alog = jnp.pad(a_log.astype(F32), (0, pad)).reshape(1, LANES)
    dsk = jnp.repeat(d_skip.astype(F32), SSM_HEAD_DIM).reshape(1, ds)
    const = lambda shape: pl.BlockSpec(shape, lambda b, c: (0,) * len(shape))
    in_specs = [pl.BlockSpec((1, L, cdim), lambda b, c: (b, c, 0)),
                pl.BlockSpec((1, L, LANES), lambda b, c: (b, c, 0)),
                pl.BlockSpec((1, L, ds), lambda b, c: (b, c, 0)),
                const((1, LANES)), const((1, ds)), const((1, ds)),
                const((D_CONV, cdim)), const((1, cdim)),
                pl.BlockSpec((1, SUBLANES, cdim), lambda b, c: (b, 0, 0)),
                pl.BlockSpec((1, SSM_STATE, ds), lambda b, c: (b, 0, 0))]
    args = [xbc, dt, z, alog, dsk, ssm_norm.reshape(1, ds), conv_w, conv_b.reshape(1, cdim), tail0, h0]
    scratch = [pltpu.VMEM((SSM_STATE, ds), F32), pltpu.VMEM((L, ds), F32),
               pltpu.VMEM((SUBLANES, cdim), F32), pltpu.VMEM((L, cdim), F32)]
    return pl.pallas_call(
        _ssd_kernel,
        grid=(bsz, t // L),
        in_specs=in_specs,
        out_specs=(pl.BlockSpec((1, L, ds), lambda b, c: (b, c, 0)),
                   pl.BlockSpec((1, SSM_STATE, ds), lambda b, c: (b, 0, 0))),
        out_shape=(jax.ShapeDtypeStruct((bsz, t, ds), BF16),
                   jax.ShapeDtypeStruct((bsz, SSM_STATE, ds), F32)),
        scratch_shapes=scratch,
        compiler_params=_cp(("parallel", "arbitrary")),
        name="ssd",
    )(*args)


def _lam(lq1, lk1, lq2, lk2, lam_init):
    return (jnp.exp(jnp.sum(lq1[...] * lk1[...], axis=-1, keepdims=True))
            - jnp.exp(jnp.sum(lq2[...] * lk2[...], axis=-1, keepdims=True)) + lam_init)


def _split_q(q):
    lane = lax.broadcasted_iota(jnp.int32, q.shape, 1)
    zero = jnp.zeros_like(q)
    return jnp.where(lane < ATT_HEAD_DIM, q, zero), jnp.where(lane >= ATT_HEAD_DIM, q, zero)


def _subln(o, sub_ref, lam_init):
    ms = jnp.mean(o * o, axis=-1, keepdims=True)
    return o * lax.rsqrt(ms + EPS) * sub_ref[...] * (1.0 - lam_init)


def _attn_prompt_kernel(lq1, lk1, lq2, lk2, sub_ref, q_ref, k_ref, v_ref, o_ref,
                        vt_s, sa_s, sb_s, m_s, acc_s, *, tq, tk, heads, lam_init):
    qi = pl.program_id(2)
    hw = 2 * ATT_HEAD_DIM
    t = k_ref.shape[1]

    @pl.when(qi == 0)
    def _():
        for g in range(heads):
            for cb in range(t // tk):
                rows = slice(cb * tk, (cb + 1) * tk)
                vt_s[g, :hw, rows] = v_ref[0, rows, g * hw:(g + 1) * hw].astype(F32).T.astype(BF16)
            vt_s[g, hw:, :] = jnp.ones((ONES_ROWS, t), BF16)

    start = pl.multiple_of(qi * tq, tq)
    kr = lax.broadcasted_iota(jnp.int32, (tk, 2 * tq), 0)
    qc = lax.broadcasted_iota(jnp.int32, (tk, 2 * tq), 1)
    qc = jnp.where(qc >= tq, qc - tq, qc)
    qq = []
    for g in range(heads):
        q1, q2 = _split_q(q_ref[0, :, g * hw:(g + 1) * hw])
        qq.append(jnp.concatenate([q1, q2], axis=0))

    def scores(off, s_ref):
        for g in range(heads):
            s_ref[g] = _nt(k_ref[0, pl.ds(off, tk), g * hw:(g + 1) * hw], qq[g])

    def consume(off, s_ref, diag):
        for g in range(heads):
            s = s_ref[g]
            if diag is not None:
                s = jnp.where(((kr + diag * tk) // CHUNK) <= (qc // CHUNK), s, NEG)
            m = m_s[g]
            mn = jnp.maximum(m, jnp.max(s, axis=0, keepdims=True))
            alpha = jnp.exp(m - mn)
            p = jnp.exp(s - mn)
            pv = jnp.dot(vt_s[g, :, pl.ds(off, tk)], p.astype(BF16), preferred_element_type=F32)
            m_s[g] = mn
            acc_s[g] = alpha * acc_s[g] + pv

    m_s[...] = jnp.full(m_s.shape, NEG, F32)
    acc_s[...] = jnp.zeros(acc_s.shape, F32)
    scores(0, sa_s)

    def pair(jp, carry):
        off = pl.multiple_of(2 * jp * tk, tq)
        scores(off + tk, sb_s)
        consume(off, sa_s, None)
        scores(off + 2 * tk, sa_s)
        consume(off + tk, sb_s, None)
        return carry

    lax.fori_loop(0, qi, pair, 0)
    scores(start + tk, sb_s)
    consume(start, sa_s, 0)
    consume(start + tk, sb_s, 1)

    lam = _lam(lq1, lk1, lq2, lk2, lam_init)
    for g in range(heads):
        an = acc_s[g, :hw, :] / acc_s[g, hw:hw + 1, :]
        o_t = an[:, :tq] - lam * an[:, tq:]
        ms = jnp.mean(o_t * o_t, axis=0, keepdims=True)
        o_t = o_t * lax.rsqrt(ms + EPS) * sub_ref[...] * (1.0 - lam_init)
        o_ref[0, :, g * hw:(g + 1) * hw] = o_t.T.astype(o_ref.dtype)


def _attn_prompt(q, k, v, lams, subln, lam_init):
    bsz, t, dq = q.shape
    tq, tk = ATT_TQ, ATT_TK
    assert t % tq == 0 and tq == 2 * tk and tk % CHUNK == 0
    hw = 2 * ATT_HEAD_DIM
    heads = ATT_HEADS_PER_STEP
    gw = heads * hw
    vec = pl.BlockSpec((1, ATT_HEAD_DIM), lambda b, h, i: (0, 0))
    return pl.pallas_call(
        functools.partial(_attn_prompt_kernel, tq=tq, tk=tk, heads=heads, lam_init=lam_init),
        grid=(bsz, dq // gw, t // tq),
        in_specs=[vec, vec, vec, vec,
                  pl.BlockSpec((hw, 1), lambda b, h, i: (0, 0)),
                  pl.BlockSpec((1, tq, gw), lambda b, h, i: (b, i, h)),
                  pl.BlockSpec((1, t, gw), lambda b, h, i: (b, 0, h)),
                  pl.BlockSpec((1, t, gw), lambda b, h, i: (b, 0, h))],
        out_specs=pl.BlockSpec((1, tq, gw), lambda b, h, i: (b, i, h)),
        out_shape=jax.ShapeDtypeStruct((bsz, t, dq), BF16),
        scratch_shapes=[pltpu.VMEM((heads, hw + ONES_ROWS, t), BF16),
                        pltpu.VMEM((heads, tk, 2 * tq), F32), pltpu.VMEM((heads, tk, 2 * tq), F32),
                        pltpu.VMEM((heads, 1, 2 * tq), F32),
                        pltpu.VMEM((heads, hw + ONES_ROWS, 2 * tq), F32)],
        compiler_params=_cp(("parallel", "parallel", "arbitrary")),
        name="attn_prompt",
    )(*lams, subln.reshape(hw, 1), q, k, v)


def _attn_decode_kernel(lq1, lk1, lq2, lk2, sub_ref, q_ref, kc_ref, vc_ref, kn_ref, vn_ref, o_ref, *, lam_init):
    lam = _lam(lq1, lk1, lq2, lk2, lam_init)
    hw = 2 * ATT_HEAD_DIM
    nheads = q_ref.shape[2] // hw
    npast = vc_ref.shape[2] // nheads
    for h in range(nheads):
        sl = slice(h * hw, (h + 1) * hw)
        q1, q2 = _split_q(q_ref[0, :, sl])
        qq = jnp.concatenate([q1, q2], axis=0)
        t = q1.shape[0]
        kct = kc_ref[0, 0, sl, :].astype(BF16)
        vc = vc_ref[0, 0, pl.ds(h, npast, stride=nheads), :].astype(BF16)
        s_c = jnp.dot(qq, kct, preferred_element_type=F32)
        s_n = _nt(qq, kn_ref[0, :, sl])
        m = jnp.maximum(jnp.max(s_c, axis=-1, keepdims=True), jnp.max(s_n, axis=-1, keepdims=True))
        p_c, p_n = jnp.exp(s_c - m), jnp.exp(s_n - m)
        l = jnp.sum(p_c, axis=-1, keepdims=True) + jnp.sum(p_n, axis=-1, keepdims=True)
        p_c, p_n = p_c / l, p_n / l
        o = (jnp.dot((p_c[:t] - lam * p_c[t:]).astype(BF16), vc, preferred_element_type=F32)
             + jnp.dot((p_n[:t] - lam * p_n[t:]).astype(BF16), vn_ref[0, :, sl], preferred_element_type=F32))
        o_ref[0, :, sl] = _subln(o, sub_ref, lam_init).astype(o_ref.dtype)


def _attn_decode(q, kn, vn, cache_k, cache_v, layer, lams, subln, lam_init):
    bsz, t, dq = q.shape
    npast = cache_v.shape[2] // (dq // (2 * ATT_HEAD_DIM))
    assert (npast + t - 1) // CHUNK <= npast // CHUNK, "decode kernel assumes every key is visible"
    hw = 2 * ATT_HEAD_DIM
    vec = pl.BlockSpec((1, ATT_HEAD_DIM), lambda b: (0, 0))
    tok = pl.BlockSpec((1, t, dq), lambda b: (b, 0, 0))
    cache = pl.BlockSpec((1, 1, npast * (dq // hw), hw), lambda b: (layer, b, 0, 0))
    cache_t = pl.BlockSpec((1, 1, dq, npast), lambda b: (layer, b, 0, 0))
    return pl.pallas_call(
        functools.partial(_attn_decode_kernel, lam_init=lam_init),
        grid=(bsz,),
        in_specs=[vec, vec, vec, vec, pl.BlockSpec((1, hw), lambda b: (0, 0)), tok, cache_t, cache, tok, tok],
        out_specs=tok,
        out_shape=jax.ShapeDtypeStruct((bsz, t, dq), BF16),
        compiler_params=_cp(("parallel",)),
        name="attn_decode",
    )(*lams, subln.reshape(1, hw), q, cache_k, cache_v, kn, vn)


def _merge_kernel(ys_ref, o_ref, wa_ref, wb_ref, g1_ref, g2_ref, b1_ref, b2_ref, out_ref):
    a = jnp.dot(ys_ref[...], wa_ref[...], preferred_element_type=F32)
    b = jnp.dot(o_ref[...], wb_ref[...], preferred_element_type=F32)
    g1 = _sigmoid(g1_ref[...] + b1_ref[...])
    g2 = _sigmoid(g2_ref[...] + b2_ref[...])
    out_ref[...] = (g1 * a + g2 * b).astype(out_ref.dtype)


def _merge(ys, o, w_ssm_out, w_att_out, g, b_gate):
    m, ka = ys.shape
    kb = o.shape[1]
    d = w_ssm_out.shape[1]
    tm, tn = _tile(m, 1024), _tile(d, 512)
    nb = d // tn
    bg = b_gate.reshape(1, 2 * d)
    return pl.pallas_call(
        _merge_kernel,
        grid=(m // tm, nb),
        in_specs=[pl.BlockSpec((tm, ka), lambda i, j: (i, 0)),
                  pl.BlockSpec((tm, kb), lambda i, j: (i, 0)),
                  pl.BlockSpec((ka, tn), lambda i, j: (0, j)),
                  pl.BlockSpec((kb, tn), lambda i, j: (0, j)),
                  pl.BlockSpec((tm, tn), lambda i, j: (i, j)),
                  pl.BlockSpec((tm, tn), lambda i, j: (i, j + nb)),
                  pl.BlockSpec((1, tn), lambda i, j: (0, j)),
                  pl.BlockSpec((1, tn), lambda i, j: (0, j + nb))],
        out_specs=pl.BlockSpec((tm, tn), lambda i, j: (i, j)),
        out_shape=jax.ShapeDtypeStruct((m, d), BF16),
        compiler_params=_cp(("parallel", "arbitrary")),
        name="merge",
    )(ys, o, w_ssm_out, w_att_out, g, g, bg, bg)


def _rope_lane_tables(pos):
    half = ROPE_DIM // 2
    inv = 1.0 / (ROPE_THETA ** (jnp.arange(0, ROPE_DIM, 2, dtype=F32) / ROPE_DIM))
    ang = pos.astype(F32)[:, None] * inv[None, :]
    cos, sin = jnp.cos(ang), jnp.sin(ang)
    d = jnp.arange(LANES) % ATT_HEAD_DIM
    lo, hi = d < half, (d >= half) & (d < ROPE_DIM)
    idx = jnp.where(hi, d - half, jnp.where(lo, d, 0))
    c = jnp.where((lo | hi)[None, :], cos[:, idx], 1.0)
    sp = jnp.where(hi[None, :], sin[:, idx], 0.0)
    sm = jnp.where(lo[None, :], -sin[:, idx], 0.0)
    return c, sp, sm


def _split_w_in(w_in, d_model):
    ds = SSM_HEADS * SSM_HEAD_DIM
    cdim = ds + 2 * SSM_GROUPS * SSM_STATE
    dqk = ATT_HEADS * 2 * ATT_HEAD_DIM
    edges = [0, ds, ds + cdim, ds + cdim + SSM_HEADS]
    edges += [edges[-1] + dqk, edges[-1] + 2 * dqk, edges[-1] + 3 * dqk, edges[-1] + 3 * dqk + 2 * d_model]
    parts = [w_in[:, :, a:b].astype(BF16) for a, b in zip(edges[:-1], edges[1:])]
    parts[2] = jnp.pad(parts[2], ((0, 0), (0, 0), (0, LANES - SSM_HEADS)))
    return parts


def _trunk(x, pos, cache_k, cache_v, state_conv, state_ssm, p):
    bsz, t, d = x.shape
    m = bsz * t
    depth = p["w13_ffn1"].shape[0]
    decode = cache_k is not None
    ds = SSM_HEADS * SSM_HEAD_DIM
    cdim = ds + 2 * SSM_GROUPS * SSM_STATE
    dqk = ATT_HEADS * 2 * ATT_HEAD_DIM
    tabs = _rope_lane_tables(pos)
    if decode:
        tabs = tuple(jnp.tile(tb, (bsz, 1)) for tb in tabs)
        period = m
    else:
        period = t
    t_pad = -(-t // SSD_L) * SSD_L
    dt_bias = jnp.pad(p["dt_bias"].astype(F32), ((0, 0), (0, LANES - SSM_HEADS)))

    x = x.reshape(m, d)
    xn = _rmsnorm(x, p["norm_ffn1"][0])
    kstack = vstack = None
    convs, ssms = [], []
    y = None
    for i in range(depth):
        lam_init = 0.8 - 0.6 * math.exp(-0.3 * i)
        h = _swiglu_up(xn, p["w13_ffn1"][i])
        x, xn = _resid_norm(h, p["w2_ffn1"][i], x, p["norm_mix"][i], 0.5)
        (z,) = _proj(xn, p["w_z"][i], (F32,))
        (xbc,) = _proj(xn, p["w_xbc"][i], (F32,))
        (dt,) = _proj(xn, p["w_dt"][i], (F32,), kind="dt", extra=(dt_bias[i:i + 1],))
        (q,) = _proj(xn, p["w_q"][i], (BF16,), kind="rope", extra=tabs, scale=ATT_HEAD_DIM ** -0.5, period=period)
        kstack, k16 = _proj(xn, p["w_k"][i], (F32, BF16), kind="rope", extra=tabs, period=period,
                            stack=(kstack, i, depth), transposed0=not decode)
        vstack, v16 = _proj(xn, p["w_v"][i], (F32, BF16), stack=(vstack, i, depth))
        (g,) = _proj(xn, p["w_g"][i], (F32,))
        xbc3 = xbc.reshape(bsz, t, cdim)
        dt3 = dt.reshape(bsz, t, LANES)
        z3 = z.reshape(bsz, t, ds)
        if t_pad != t:
            padt = ((0, 0), (0, t_pad - t), (0, 0))
            xbc_in, dt_in, z_in = jnp.pad(xbc3, padt), jnp.pad(dt3, padt), jnp.pad(z3, padt)
        else:
            xbc_in, dt_in, z_in = xbc3, dt3, z3
        if decode:
            prev = state_conv[i].astype(F32)
            h0 = jnp.transpose(state_ssm[i].astype(F32).reshape(bsz, ds, SSM_STATE), (0, 2, 1))
        else:
            prev = jnp.zeros((bsz, D_CONV - 1, cdim), F32)
            h0 = jnp.zeros((bsz, SSM_STATE, ds), F32)
        tail0 = jnp.pad(prev, ((0, 0), (SUBLANES - (D_CONV - 1), 0), (0, 0)))
        ys, h_t = _ssd(xbc_in, dt_in, z_in, p["a_log"][i], p["d_skip"][i], p["ssm_norm"][i],
                       p["conv_w"][i], p["conv_b"][i], tail0, h0)
        ys = ys[:, :t].reshape(m, ds)
        convs.append(jnp.concatenate([prev, xbc3], axis=1)[:, t:])
        ssms.append(jnp.transpose(h_t, (0, 2, 1)).reshape(bsz, SSM_HEADS, SSM_HEAD_DIM, SSM_STATE))
        lams = tuple(p[n][i].reshape(1, ATT_HEAD_DIM).astype(F32)
                     for n in ("lambda_q1", "lambda_k1", "lambda_q2", "lambda_k2"))
        q3, k3, v3 = (a.reshape(bsz, t, dqk) for a in (q, k16, v16))
        if decode:
            o = _attn_decode(q3, k3, v3, cache_k, cache_v, i, lams, p["subln"][i], lam_init)
        else:
            o = _attn_prompt(q3, k3, v3, lams, p["subln"][i], lam_init)
        mix = _merge(ys, o.reshape(m, dqk), p["w_ssm_out"][i], p["w_att_out"][i], g, p["b_gate"][i])
        x, xn = _resid_norm(mix, p["w_o"][i], x, p["norm_ffn2"][i], 1.0)
        h = _swiglu_up(xn, p["w13_ffn2"][i])
        if i + 1 < depth:
            x, xn = _resid_norm(h, p["w2_ffn2"][i], x, p["norm_ffn1"][i + 1], 0.5)
        else:
            y = _resid_norm(h, p["w2_ffn2"][i], x, p["norm_final"], 0.5, final=True)
    if decode:
        k_all = kstack.reshape(depth, bsz, t, ATT_HEADS, 2, ATT_HEAD_DIM)
    else:
        k_all = jnp.transpose(kstack.reshape(depth, bsz, ATT_HEADS, 2, ATT_HEAD_DIM, t), (0, 1, 5, 2, 3, 4))
    v_all = vstack.reshape(depth, bsz, t, ATT_HEADS, 2 * ATT_HEAD_DIM)
    return y.reshape(bsz, t, d), k_all, v_all, jnp.stack(convs), jnp.stack(ssms)


def kernel(x_prompt, x_sample, cache_k, cache_v, state_conv, state_ssm, norm_ffn1, w13_ffn1, w2_ffn1, norm_mix, w_in, conv_w, conv_b, dt_bias, a_log, d_skip, ssm_norm, w_ssm_out, lambda_q1, lambda_k1, lambda_q2, lambda_k2, subln, w_att_out, b_gate, w_o, norm_ffn2, w13_ffn2, w2_ffn2, norm_final):
    d_model = x_prompt.shape[-1]
    w_z, w_xbc, w_dt, w_q, w_k, w_v, w_g = _split_w_in(w_in, d_model)
    p = dict(
        norm_ffn1=norm_ffn1, w13_ffn1=w13_ffn1.astype(BF16), w2_ffn1=w2_ffn1.astype(BF16), norm_mix=norm_mix,
        w_z=w_z, w_xbc=w_xbc, w_dt=w_dt, w_q=w_q, w_k=w_k, w_v=w_v, w_g=w_g,
        conv_w=conv_w, conv_b=conv_b, dt_bias=dt_bias, a_log=a_log, d_skip=d_skip, ssm_norm=ssm_norm,
        w_ssm_out=w_ssm_out.astype(BF16), lambda_q1=lambda_q1, lambda_k1=lambda_k1, lambda_q2=lambda_q2,
        lambda_k2=lambda_k2, subln=subln, w_att_out=w_att_out.astype(BF16), b_gate=b_gate,
        w_o=w_o.astype(BF16), norm_ffn2=norm_ffn2, w13_ffn2=w13_ffn2.astype(BF16), w2_ffn2=w2_ffn2.astype(BF16),
        norm_final=norm_final)
    depth, dec_b, n_past = cache_k.shape[:3]
    dqk = ATT_HEADS * 2 * ATT_HEAD_DIM
    y_p, k_p, v_p, conv_p, ssm_p = _trunk(x_prompt, jnp.arange(x_prompt.shape[1]), None, None, None, None, p)
    ck = jnp.transpose(cache_k, (0, 1, 3, 4, 5, 2)).reshape(depth, dec_b, dqk, n_past)
    cv = cache_v.reshape(depth, dec_b, n_past * ATT_HEADS, 2 * ATT_HEAD_DIM)
    y_s, k_s, v_s, conv_s, ssm_s = _trunk(x_sample, n_past + jnp.arange(x_sample.shape[1]), ck, cv,
                                          state_conv, state_ssm, p)
    return (y_p, y_s, k_p, v_p, conv_p, ssm_p, k_s, v_s, conv_s, ssm_s)
```

```python
import functools
import math

import jax
import jax.numpy as jnp
from jax import lax
from jax.experimental import pallas as pl
from jax.experimental.pallas import tpu as pltpu

F32 = jnp.float32
BF16 = jnp.bfloat16

CHUNK = 64
SSM_HEADS = 32
SSM_HEAD_DIM = 64
SSM_GROUPS = 8
SSM_STATE = 128
D_CONV = 4
ATT_HEADS = 16
ATT_HEAD_DIM = 64
ROPE_DIM = ATT_HEAD_DIM // 4
ROPE_THETA = 500000.0
EPS = 1e-6

LANES = 128
SUBLANES = 8
SSD_L = 128
ATT_TQ = 512
ATT_TK = 256
ATT_HEADS_PER_STEP = 4
ONES_ROWS = 16
VMEM_LIMIT = 56 * 1024 * 1024
NEG = -1e30
LOG2E = math.log2(math.e)


def _cp(sem, vmem=VMEM_LIMIT):
    return pltpu.CompilerParams(dimension_semantics=sem, vmem_limit_bytes=vmem)


def _tile(dim, pref):
    t = min(dim, pref)
    while dim % t:
        t //= 2
    return t


def _sigmoid(x):
    return 1.0 / (1.0 + jnp.exp(-x))


def _silu(x):
    h = 0.5 * x
    return h + h * jnp.tanh(h)


def _nt(a, b):
    return lax.dot_general(a, b, (((1,), (1,)), ((), ())), preferred_element_type=F32)


def _tn(a, b):
    return lax.dot_general(a, b, (((0,), (0,)), ((), ())), preferred_element_type=F32)


def _rmsnorm_kernel(x_ref, g_ref, o_ref):
    x = x_ref[...]
    ms = jnp.mean(x * x, axis=-1, keepdims=True)
    o_ref[...] = (x * lax.rsqrt(ms + EPS) * g_ref[...]).astype(o_ref.dtype)


def _rmsnorm(x, g):
    m, d = x.shape
    tm = _tile(m, 512)
    return pl.pallas_call(
        _rmsnorm_kernel,
        grid=(m // tm,),
        in_specs=[pl.BlockSpec((tm, d), lambda i: (i, 0)), pl.BlockSpec((1, d), lambda i: (0, 0))],
        out_specs=pl.BlockSpec((tm, d), lambda i: (i, 0)),
        out_shape=jax.ShapeDtypeStruct((m, d), BF16),
        compiler_params=_cp(("parallel",)),
        name="rmsnorm",
    )(x, g.reshape(1, d))


def _swiglu_kernel(x_ref, w1_ref, w3_ref, o_ref):
    x = x_ref[...]
    a = jnp.dot(x, w1_ref[...], preferred_element_type=F32)
    b = jnp.dot(x, w3_ref[...], preferred_element_type=F32)
    o_ref[...] = (a * _sigmoid(a) * b).astype(o_ref.dtype)


def _swiglu_up(xn, w13, layer):
    m, d = xn.shape
    f = w13.shape[2] // 2
    tm, tn = _tile(m, 1024), _tile(f, 512)
    nb = f // tn
    return pl.pallas_call(
        _swiglu_kernel,
        grid=(m // tm, nb),
        in_specs=[pl.BlockSpec((tm, d), lambda i, j: (i, 0)),
                  pl.BlockSpec((None, d, tn), lambda i, j: (layer, 0, j)),
                  pl.BlockSpec((None, d, tn), lambda i, j: (layer, 0, j + nb))],
        out_specs=pl.BlockSpec((tm, tn), lambda i, j: (i, j)),
        out_shape=jax.ShapeDtypeStruct((m, f), BF16),
        compiler_params=_cp(("parallel", "arbitrary")),
        name="swiglu_up",
    )(xn, w13, w13)


def _resid_norm_kernel(h_ref, w_ref, x_ref, g_ref, *o_refs, scale, final):
    acc = jnp.dot(h_ref[...], w_ref[...], preferred_element_type=F32)
    xnew = x_ref[...] + scale * acc
    ms = jnp.mean(xnew * xnew, axis=-1, keepdims=True)
    y = xnew * lax.rsqrt(ms + EPS) * g_ref[...]
    if final:
        o_refs[0][...] = y
    else:
        o_refs[0][...] = xnew
        o_refs[1][...] = y.astype(BF16)


def _resid_norm(h, w, layer, x, g, scale, final=False):
    m, k = h.shape
    d = w.shape[2]
    tm = _tile(m, 256)
    row = pl.BlockSpec((tm, d), lambda i: (i, 0))
    if final:
        out_shape, out_specs = jax.ShapeDtypeStruct((m, d), F32), row
    else:
        out_shape = (jax.ShapeDtypeStruct((m, d), F32), jax.ShapeDtypeStruct((m, d), BF16))
        out_specs = (row, row)
    return pl.pallas_call(
        functools.partial(_resid_norm_kernel, scale=scale, final=final),
        grid=(m // tm,),
        in_specs=[pl.BlockSpec((tm, k), lambda i: (i, 0)),
                  pl.BlockSpec((None, k, d), lambda i: (layer, 0, 0), pipeline_mode=pl.Buffered(1)),
                  row,
                  pl.BlockSpec((1, d), lambda i: (0, 0))],
        out_specs=out_specs,
        out_shape=out_shape,
        compiler_params=_cp(("parallel",)),
        name="resid_norm",
    )(h, w, x, g.reshape(1, d))


def _proj_plain_kernel(x_ref, w_ref, *o_refs):
    acc = jnp.dot(x_ref[...], w_ref[...], preferred_element_type=F32)
    for o in o_refs:
        o[...] = acc.astype(o.dtype)


def _proj_dt_kernel(x_ref, w_ref, b_ref, o_ref):
    v = jnp.dot(x_ref[...], w_ref[...], preferred_element_type=F32) + b_ref[...]
    o_ref[...] = jnp.maximum(v, 0.0) + jnp.log1p(jnp.exp(-jnp.abs(v)))


def _proj_rope_kernel(x_ref, w_ref, c_ref, sp_ref, sm_ref, *o_refs, scale, transposed0):
    acc = jnp.dot(x_ref[...], w_ref[...], preferred_element_type=F32)
    c, sp, sm = c_ref[...], sp_ref[...], sm_ref[...]
    half = ROPE_DIM // 2
    for cb in range(acc.shape[1] // LANES):
        sl = slice(cb * LANES, (cb + 1) * LANES)
        y = acc[:, sl]
        r = y * c + pltpu.roll(y, half, 1) * sp + pltpu.roll(y, LANES - half, 1) * sm
        if scale != 1.0:
            r = r * scale
        for n, o in enumerate(o_refs):
            if n == 0 and transposed0:
                o[sl, :] = r.T.astype(o.dtype)
            else:
                o[:, sl] = r.astype(o.dtype)


def _proj(xn, w, wlayer, out_dtypes, kind="plain", extra=(), scale=1.0, period=None, stack=None,
          transposed0=False, cols=None):
    m, d = xn.shape
    col0, n = cols if cols is not None else (0, w.shape[2])
    tm, tn = _tile(m, 1024), _tile(n, 1024)
    assert col0 % tn == 0
    jb = col0 // tn
    if kind == "rope":
        tm = _tile(period, tm)
    in_specs = [pl.BlockSpec((tm, d), lambda i, j: (i, 0)), pl.BlockSpec((None, d, tn), lambda i, j: (wlayer, 0, j + jb))]
    if kind == "plain":
        body = _proj_plain_kernel
    elif kind == "dt":
        body = _proj_dt_kernel
        in_specs.append(pl.BlockSpec((1, tn), lambda i, j: (0, j)))
    else:
        body = functools.partial(_proj_rope_kernel, scale=scale, transposed0=transposed0)
        nper = period // tm
        in_specs += [pl.BlockSpec((tm, LANES), lambda i, j: (i % nper, 0))] * 3
    out_shape = [jax.ShapeDtypeStruct((m, n), dt) for dt in out_dtypes]
    out_specs = [pl.BlockSpec((tm, tn), lambda i, j: (i, j)) for _ in out_dtypes]
    args = [xn, w, *extra]
    aliases = {}
    if stack is not None:
        buf, layer, depth = stack
        if transposed0:
            out_shape[0] = jax.ShapeDtypeStruct((depth, m // period, n, period), out_dtypes[0])
            out_specs[0] = pl.BlockSpec((None, None, tn, tm), lambda i, j: (layer, i // nper, j, i % nper))
        else:
            out_shape[0] = jax.ShapeDtypeStruct((depth, m, n), out_dtypes[0])
            out_specs[0] = pl.BlockSpec((None, tm, tn), lambda i, j: (layer, i, j))
        if buf is not None:
            n_in = len(args)
            inner = body
            body = lambda *refs: inner(*refs[:n_in], *refs[n_in + 1:])
            in_specs.append(pl.BlockSpec(memory_space=pl.ANY))
            args.append(buf)
            aliases = {n_in: 0}
    return pl.pallas_call(
        body,
        grid=(m // tm, n // tn),
        in_specs=in_specs,
        out_specs=tuple(out_specs),
        out_shape=tuple(out_shape),
        input_output_aliases=aliases,
        compiler_params=_cp(("parallel", "arbitrary")),
        name="proj_" + kind,
    )(*args)


def _conv_silu(read_rows, prev8, cw_ref, cb_ref, sl, rows, width):
    row8 = lax.broadcasted_iota(jnp.int32, (SUBLANES, width), 0)
    x_top = read_rows(0, SUBLANES)
    w_last = cw_ref[D_CONV - 1:D_CONV, sl]
    top = cb_ref[:, sl] + x_top * w_last
    rest = cb_ref[:, sl] + read_rows(SUBLANES, rows) * w_last
    for k in range(1, D_CONV):
        wk = cw_ref[D_CONV - 1 - k:D_CONV - k, sl]
        first = jnp.where(row8 < k, pltpu.roll(prev8, k, 0), pltpu.roll(x_top, k, 0))
        top = top + first * wk
        rest = rest + read_rows(SUBLANES - k, rows - k) * wk
    conv = jnp.concatenate([top, rest], axis=0)
    return _silu(conv)


def _ssd_kernel(xbc_ref, dt_ref, z_ref, alog_ref, dsk_ref, nw_ref, cw_ref, cb_ref, tail0_ref, h0_ref,
                y_ref, hout_ref, h_s, y_s, tail_s, act_s):
    L, N, P = SSD_L, SSM_STATE, SSM_HEAD_DIM
    ds = SSM_HEADS * P
    cdim = ds + 2 * SSM_GROUPS * N
    c = pl.program_id(1)

    @pl.when(c == 0)
    def _():
        tail_s[...] = tail0_ref[0]
        h_s[...] = h0_ref[0]

    cw = 512
    for cb in range(cdim // cw):
        sl = slice(cb * cw, (cb + 1) * cw)
        act_s[:, sl] = _conv_silu(lambda a, b: xbc_ref[0, a:b, sl], tail_s[:, sl], cw_ref, cb_ref, sl, L, cw)
        tail_s[:, sl] = xbc_ref[0, L - SUBLANES:, sl]
    act = lambda sl: act_s[:, sl]

    dtv = dt_ref[0]
    da = dtv * (-jnp.exp(alog_ref[...]) * LOG2E)
    ri = lax.broadcasted_iota(jnp.int32, (L, L), 0)
    ci = lax.broadcasted_iota(jnp.int32, (L, L), 1)
    tri = ri >= ci
    acs = jnp.dot(tri.astype(F32), da, precision=lax.Precision.HIGHEST, preferred_element_type=F32)
    acs_t = acs.T
    acs_end = acs[L - 1:L, :]
    first = lax.broadcasted_iota(jnp.int32, (L, LANES), 1) < P
    first1 = lax.broadcasted_iota(jnp.int32, (1, LANES), 1) < P
    heads_per_group = SSM_HEADS // SSM_GROUPS
    pairs_per_group = heads_per_group // 2

    for g in range(SSM_GROUPS):
        bg = act(slice(ds + g * N, ds + (g + 1) * N))
        cg = act(slice(ds + (SSM_GROUPS + g) * N, ds + (SSM_GROUPS + g + 1) * N))
        sc = _nt(cg.astype(BF16), bg.astype(BF16))
        heads = [g * heads_per_group + r for r in range(heads_per_group)]
        colbs = [jnp.broadcast_to(acs[:, h:h + 1], (L, LANES)) for h in heads]
        dtcolbs = [jnp.broadcast_to(dtv[:, h:h + 1], (L, LANES)) for h in heads]
        lhss, bws = [], []
        for r, h in enumerate(heads):
            colb = colbs[r]
            dec = jnp.exp2(jnp.where(tri, colb - acs_t[h:h + 1, :], NEG))
            mh = (sc * dec).astype(BF16)
            ce = (cg * jnp.exp2(colb)).astype(BF16)
            lhss.append(jnp.concatenate([mh, ce], axis=1))
            bws.append((bg * jnp.exp2(acs_end[:, h:h + 1] - colb)).astype(BF16))
        for pr in range(pairs_per_group):
            pair = g * pairs_per_group + pr
            psl = slice(pair * LANES, (pair + 1) * LANES)
            r0, r1 = 2 * pr, 2 * pr + 1
            ht_pair = h_s[:, psl]
            xdt = (act(psl) * jnp.where(first, dtcolbs[r0], dtcolbs[r1])).astype(BF16)
            rhs = jnp.concatenate([xdt, ht_pair.astype(BF16)], axis=0)
            y_s[:, psl] = jnp.where(first, jnp.dot(lhss[r0], rhs, preferred_element_type=F32),
                                    jnp.dot(lhss[r1], rhs, preferred_element_type=F32))
            upd = jnp.where(first, _tn(bws[r0], xdt), _tn(bws[r1], xdt))
            dec_lane = jnp.where(first1, jnp.exp2(acs_end[:, heads[r0]:heads[r0] + 1]),
                                 jnp.exp2(acs_end[:, heads[r1]:heads[r1] + 1]))
            h_s[:, psl] = ht_pair * dec_lane + upd

    gs = ds // SSM_GROUPS
    for g in range(SSM_GROUPS):
        sl = slice(g * gs, (g + 1) * gs)
        z = z_ref[0, :, sl]
        y = (y_s[:, sl] + dsk_ref[:, sl] * act(sl)) * _silu(z)
        ms = jnp.mean(y * y, axis=-1, keepdims=True)
        y_ref[0, :, sl] = (y * lax.rsqrt(ms + EPS) * nw_ref[:, sl]).astype(y_ref.dtype)

    @pl.when(c == pl.num_programs(1) - 1)
    def _():
        hout_ref[0] = h_s[...]


def _ssd(xbc, dt, z, a_log, d_skip, ssm_norm, conv_w, conv_b, tail0, h0):
    bsz, t, cdim = xbc.shape
    ds = SSM_HEADS * SSM_HEAD_DIM
    L = SSD_L
    assert SSD_L == SSM_STATE == LANES and t % L == 0
    pad = LANES - SSM_HEADS
    alog = jnp.pad(a_log.astype(F32), (0, pad)).reshape(1, LANES)
    dsk = jnp.repeat(d_skip.astype(F32), SSM_HEAD_DIM).reshape(1, ds)
    const = lambda shape: pl.BlockSpec(shape, lambda b, c: (0,) * len(shape))
    in_specs = [pl.BlockSpec((1, L, cdim), lambda b, c: (b, c, 0)),
                pl.BlockSpec((1, L, LANES), lambda b, c: (b, c, 0)),
                pl.BlockSpec((1, L, ds), lambda b, c: (b, c, 0)),
                const((1, LANES)), const((1, ds)), const((1, ds)),
                const((D_CONV, cdim)), const((1, cdim)),
                pl.BlockSpec((1, SUBLANES, cdim), lambda b, c: (b, 0, 0)),
                pl.BlockSpec((1, SSM_STATE, ds), lambda b, c: (b, 0, 0))]
    args = [xbc, dt, z, alog, dsk, ssm_norm.reshape(1, ds), conv_w, conv_b.reshape(1, cdim), tail0, h0]
    scratch = [pltpu.VMEM((SSM_STATE, ds), F32), pltpu.VMEM((L, ds), F32),
               pltpu.VMEM((SUBLANES, cdim), F32), pltpu.VMEM((L, cdim), F32)]
    return pl.pallas_call(
        _ssd_kernel,
        grid=(bsz, t // L),
        in_specs=in_specs,
        out_specs=(pl.BlockSpec((1, L, ds), lambda b, c: (b, c, 0)),
                   pl.BlockSpec((1, SSM_STATE, ds), lambda b, c: (b, 0, 0))),
        out_shape=(jax.ShapeDtypeStruct((bsz, t, ds), BF16),
                   jax.ShapeDtypeStruct((bsz, SSM_STATE, ds), F32)),
        scratch_shapes=scratch,
        compiler_params=_cp(("parallel", "arbitrary")),
        name="ssd",
    )(*args)


def _lam(lq1, lk1, lq2, lk2, lam_init):
    return (jnp.exp(jnp.sum(lq1[...] * lk1[...], axis=-1, keepdims=True))
            - jnp.exp(jnp.sum(lq2[...] * lk2[...], axis=-1, keepdims=True)) + lam_init)


def _split_q(q):
    lane = lax.broadcasted_iota(jnp.int32, q.shape, 1)
    zero = jnp.zeros_like(q)
    return jnp.where(lane < ATT_HEAD_DIM, q, zero), jnp.where(lane >= ATT_HEAD_DIM, q, zero)


def _subln(o, sub_ref, lam_init):
    ms = jnp.mean(o * o, axis=-1, keepdims=True)
    return o * lax.rsqrt(ms + EPS) * sub_ref[...] * (1.0 - lam_init)


def _attn_prompt_kernel(lq1, lk1, lq2, lk2, sub_ref, q_ref, k_ref, v_ref, o_ref,
                        vt_s, sa_s, sb_s, m_s, acc_s, *, tq, tk, heads, lam_init):
    qi = pl.program_id(2)
    hw = 2 * ATT_HEAD_DIM
    t = k_ref.shape[1]

    @pl.when(qi == 0)
    def _():
        for g in range(heads):
            for cb in range(t // tk):
                rows = slice(cb * tk, (cb + 1) * tk)
                vt_s[g, :hw, rows] = v_ref[0, rows, g * hw:(g + 1) * hw].astype(F32).T.astype(BF16)
            vt_s[g, hw:, :] = jnp.ones((ONES_ROWS, t), BF16)

    start = pl.multiple_of(qi * tq, tq)
    kr = lax.broadcasted_iota(jnp.int32, (tk, 2 * tq), 0)
    qc = lax.broadcasted_iota(jnp.int32, (tk, 2 * tq), 1)
    qc = jnp.where(qc >= tq, qc - tq, qc)
    qq = []
    for g in range(heads):
        q1, q2 = _split_q(q_ref[0, :, g * hw:(g + 1) * hw])
        qq.append(jnp.concatenate([q1, q2], axis=0))

    def scores(off, s_ref):
        for g in range(heads):
            s_ref[g] = _nt(k_ref[0, pl.ds(off, tk), g * hw:(g + 1) * hw], qq[g])

    def consume(off, s_ref, diag):
        for g in range(heads):
            s = s_ref[g]
            if diag is not None:
                s = jnp.where(((kr + diag * tk) // CHUNK) <= (qc // CHUNK), s, NEG)
            m = m_s[g]
            mn = jnp.maximum(m, jnp.max(s, axis=0, keepdims=True))
            alpha = jnp.exp(m - mn)
            p = jnp.exp(s - mn)
            pv = jnp.dot(vt_s[g, :, pl.ds(off, tk)], p.astype(BF16), preferred_element_type=F32)
            m_s[g] = mn
            acc_s[g] = alpha * acc_s[g] + pv

    m_s[...] = jnp.full(m_s.shape, NEG, F32)
    acc_s[...] = jnp.zeros(acc_s.shape, F32)
    scores(0, sa_s)

    def pair(jp, carry):
        off = pl.multiple_of(2 * jp * tk, tq)
        scores(off + tk, sb_s)
        consume(off, sa_s, None)
        scores(off + 2 * tk, sa_s)
        consume(off + tk, sb_s, None)
        return carry

    lax.fori_loop(0, qi, pair, 0)
    scores(start + tk, sb_s)
    consume(start, sa_s, 0)
    consume(start + tk, sb_s, 1)

    lam = _lam(lq1, lk1, lq2, lk2, lam_init)
    for g in range(heads):
        an = acc_s[g, :hw, :] / acc_s[g, hw:hw + 1, :]
        o_t = an[:, :tq] - lam * an[:, tq:]
        ms = jnp.mean(o_t * o_t, axis=0, keepdims=True)
        o_t = o_t * lax.rsqrt(ms + EPS) * sub_ref[...] * (1.0 - lam_init)
        o_ref[0, :, g * hw:(g + 1) * hw] = o_t.T.astype(o_ref.dtype)


def _attn_prompt(q, k, v, lams, subln, lam_init):
    bsz, t, dq = q.shape
    tq, tk = ATT_TQ, ATT_TK
    assert t % tq == 0 and tq == 2 * tk and tk % CHUNK == 0
    hw = 2 * ATT_HEAD_DIM
    heads = ATT_HEADS_PER_STEP
    gw = heads * hw
    vec = pl.BlockSpec((1, ATT_HEAD_DIM), lambda b, h, i: (0, 0))
    return pl.pallas_call(
        functools.partial(_attn_prompt_kernel, tq=tq, tk=tk, heads=heads, lam_init=lam_init),
        grid=(bsz, dq // gw, t // tq),
        in_specs=[vec, vec, vec, vec,
                  pl.BlockSpec((hw, 1), lambda b, h, i: (0, 0)),
                  pl.BlockSpec((1, tq, gw), lambda b, h, i: (b, i, h)),
                  pl.BlockSpec((1, t, gw), lambda b, h, i: (b, 0, h)),
                  pl.BlockSpec((1, t, gw), lambda b, h, i: (b, 0, h))],
        out_specs=pl.BlockSpec((1, tq, gw), lambda b, h, i: (b, i, h)),
        out_shape=jax.ShapeDtypeStruct((bsz, t, dq), BF16),
        scratch_shapes=[pltpu.VMEM((heads, hw + ONES_ROWS, t), BF16),
                        pltpu.VMEM((heads, tk, 2 * tq), F32), pltpu.VMEM((heads, tk, 2 * tq), F32),
                        pltpu.VMEM((heads, 1, 2 * tq), F32),
                        pltpu.VMEM((heads, hw + ONES_ROWS, 2 * tq), F32)],
        compiler_params=_cp(("parallel", "parallel", "arbitrary")),
        name="attn_prompt",
    )(*lams, subln.reshape(hw, 1), q, k, v)


def _attn_decode_kernel(lq1, lk1, lq2, lk2, sub_ref, q_ref, kc_ref, vc_ref, kn_ref, vn_ref, o_ref, *, lam_init):
    lam = _lam(lq1, lk1, lq2, lk2, lam_init)
    hw = 2 * ATT_HEAD_DIM
    nheads = q_ref.shape[2] // hw
    npast = vc_ref.shape[2] // nheads
    for h in range(nheads):
        sl = slice(h * hw, (h + 1) * hw)
        q1, q2 = _split_q(q_ref[0, :, sl])
        qq = jnp.concatenate([q1, q2], axis=0)
        t = q1.shape[0]
        kct = kc_ref[0, 0, sl, :].astype(BF16)
        vc = vc_ref[0, 0, pl.ds(h, npast, stride=nheads), :].astype(BF16)
        s_c = jnp.dot(qq, kct, preferred_element_type=F32)
        s_n = _nt(qq, kn_ref[0, :, sl])
        m = jnp.maximum(jnp.max(s_c, axis=-1, keepdims=True), jnp.max(s_n, axis=-1, keepdims=True))
        p_c, p_n = jnp.exp(s_c - m), jnp.exp(s_n - m)
        l = jnp.sum(p_c, axis=-1, keepdims=True) + jnp.sum(p_n, axis=-1, keepdims=True)
        p_c, p_n = p_c / l, p_n / l
        o = (jnp.dot((p_c[:t] - lam * p_c[t:]).astype(BF16), vc, preferred_element_type=F32)
             + jnp.dot((p_n[:t] - lam * p_n[t:]).astype(BF16), vn_ref[0, :, sl], preferred_element_type=F32))
        o_ref[0, :, sl] = _subln(o, sub_ref, lam_init).astype(o_ref.dtype)


def _attn_decode(q, kn, vn, cache_k, cache_v, layer, lams, subln, lam_init):
    bsz, t, dq = q.shape
    npast = cache_v.shape[2] // (dq // (2 * ATT_HEAD_DIM))
    assert (npast + t - 1) // CHUNK <= npast // CHUNK, "decode kernel assumes every key is visible"
    hw = 2 * ATT_HEAD_DIM
    vec = pl.BlockSpec((1, ATT_HEAD_DIM), lambda b: (0, 0))
    tok = pl.BlockSpec((1, t, dq), lambda b: (b, 0, 0))
    cache = pl.BlockSpec((1, 1, npast * (dq // hw), hw), lambda b: (layer, b, 0, 0))
    cache_t = pl.BlockSpec((1, 1, dq, npast), lambda b: (layer, b, 0, 0))
    return pl.pallas_call(
        functools.partial(_attn_decode_kernel, lam_init=lam_init),
        grid=(bsz,),
        in_specs=[vec, vec, vec, vec, pl.BlockSpec((1, hw), lambda b: (0, 0)), tok, cache_t, cache, tok, tok],
        out_specs=tok,
        out_shape=jax.ShapeDtypeStruct((bsz, t, dq), BF16),
        compiler_params=_cp(("parallel",)),
        name="attn_decode",
    )(*lams, subln.reshape(1, hw), q, cache_k, cache_v, kn, vn)


def _merge_kernel(ys_ref, o_ref, wa_ref, wb_ref, g1_ref, g2_ref, b1_ref, b2_ref, out_ref):
    a = jnp.dot(ys_ref[...], wa_ref[...], preferred_element_type=F32)
    b = jnp.dot(o_ref[...], wb_ref[...], preferred_element_type=F32)
    g1 = _sigmoid(g1_ref[...] + b1_ref[...])
    g2 = _sigmoid(g2_ref[...] + b2_ref[...])
    out_ref[...] = (g1 * a + g2 * b).astype(out_ref.dtype)


def _merge(ys, o, w_ssm_out, w_att_out, layer, g, b_gate):
    m, ka = ys.shape
    kb = o.shape[1]
    d = w_ssm_out.shape[2]
    tm, tn = _tile(m, 1024), _tile(d, 512)
    nb = d // tn
    bg = b_gate.reshape(1, 2 * d)
    return pl.pallas_call(
        _merge_kernel,
        grid=(m // tm, nb),
        in_specs=[pl.BlockSpec((tm, ka), lambda i, j: (i, 0)),
                  pl.BlockSpec((tm, kb), lambda i, j: (i, 0)),
                  pl.BlockSpec((None, ka, tn), lambda i, j: (layer, 0, j)),
                  pl.BlockSpec((None, kb, tn), lambda i, j: (layer, 0, j)),
                  pl.BlockSpec((tm, tn), lambda i, j: (i, j)),
                  pl.BlockSpec((tm, tn), lambda i, j: (i, j + nb)),
                  pl.BlockSpec((1, tn), lambda i, j: (0, j)),
                  pl.BlockSpec((1, tn), lambda i, j: (0, j + nb))],
        out_specs=pl.BlockSpec((tm, tn), lambda i, j: (i, j)),
        out_shape=jax.ShapeDtypeStruct((m, d), BF16),
        compiler_params=_cp(("parallel", "arbitrary")),
        name="merge",
    )(ys, o, w_ssm_out, w_att_out, g, g, bg, bg)


def _rope_lane_tables(pos):
    half = ROPE_DIM // 2
    inv = 1.0 / (ROPE_THETA ** (jnp.arange(0, ROPE_DIM, 2, dtype=F32) / ROPE_DIM))
    ang = pos.astype(F32)[:, None] * inv[None, :]
    cos, sin = jnp.cos(ang), jnp.sin(ang)
    d = jnp.arange(LANES) % ATT_HEAD_DIM
    lo, hi = d < half, (d >= half) & (d < ROPE_DIM)
    idx = jnp.where(hi, d - half, jnp.where(lo, d, 0))
    c = jnp.where((lo | hi)[None, :], cos[:, idx], 1.0)
    sp = jnp.where(hi[None, :], sin[:, idx], 0.0)
    sm = jnp.where(lo[None, :], -sin[:, idx], 0.0)
    return c, sp, sm


def _split_w_in(w_in, d_model):
    ds = SSM_HEADS * SSM_HEAD_DIM
    cdim = ds + 2 * SSM_GROUPS * SSM_STATE
    dqk = ATT_HEADS * 2 * ATT_HEAD_DIM
    edges = [0, ds, ds + cdim, ds + cdim + SSM_HEADS]
    edges += [edges[-1] + dqk, edges[-1] + 2 * dqk, edges[-1] + 3 * dqk, edges[-1] + 3 * dqk + 2 * d_model]
    w16 = w_in.astype(BF16)
    parts = [w16[:, :, a:b] for a, b in zip(edges[2:-1], edges[3:])]
    parts[0] = jnp.pad(parts[0], ((0, 0), (0, 0), (0, LANES - SSM_HEADS)))
    return [w16] + parts


def _trunk(x, pos, cache_k, cache_v, state_conv, state_ssm, p):
    bsz, t, d = x.shape
    m = bsz * t
    depth = p["w13_ffn1"].shape[0]
    decode = cache_k is not None
    ds = SSM_HEADS * SSM_HEAD_DIM
    cdim = ds + 2 * SSM_GROUPS * SSM_STATE
    dqk = ATT_HEADS * 2 * ATT_HEAD_DIM
    tabs = _rope_lane_tables(pos)
    if decode:
        tabs = tuple(jnp.tile(tb, (bsz, 1)) for tb in tabs)
        period = m
    else:
        period = t
    t_pad = -(-t // SSD_L) * SSD_L
    dt_bias = jnp.pad(p["dt_bias"].astype(F32), ((0, 0), (0, LANES - SSM_HEADS)))

    x = x.reshape(m, d)
    xn = _rmsnorm(x, p["norm_ffn1"][0])
    kstack = vstack = None
    convs, ssms = [], []
    y = None
    for i in range(depth):
        lam_init = 0.8 - 0.6 * math.exp(-0.3 * i)
        h = _swiglu_up(xn, p["w13_ffn1"], i)
        x, xn = _resid_norm(h, p["w2_ffn1"], i, x, p["norm_mix"][i], 0.5)
        (z,) = _proj(xn, p["w_in"], i, (F32,), cols=(0, ds))
        (xbc,) = _proj(xn, p["w_in"], i, (F32,), cols=(ds, cdim))
        (dt,) = _proj(xn, p["w_dt"], i, (F32,), kind="dt", extra=(dt_bias[i:i + 1],))
        (q,) = _proj(xn, p["w_q"], i, (BF16,), kind="rope", extra=tabs, scale=ATT_HEAD_DIM ** -0.5, period=period)
        kstack, k16 = _proj(xn, p["w_k"], i, (F32, BF16), kind="rope", extra=tabs, period=period,
                            stack=(kstack, i, depth), transposed0=not decode)
        vstack, v16 = _proj(xn, p["w_v"], i, (F32, BF16), stack=(vstack, i, depth))
        (g,) = _proj(xn, p["w_g"], i, (F32,))
        xbc3 = xbc.reshape(bsz, t, cdim)
        dt3 = dt.reshape(bsz, t, LANES)
        z3 = z.reshape(bsz, t, ds)
        if t_pad != t:
            padt = ((0, 0), (0, t_pad - t), (0, 0))
            xbc_in, dt_in, z_in = jnp.pad(xbc3, padt), jnp.pad(dt3, padt), jnp.pad(z3, padt)
        else:
            xbc_in, dt_in, z_in = xbc3, dt3, z3
        if decode:
            prev = state_conv[i].astype(F32)
            h0 = jnp.transpose(state_ssm[i].astype(F32).reshape(bsz, ds, SSM_STATE), (0, 2, 1))
        else:
            prev = jnp.zeros((bsz, D_CONV - 1, cdim), F32)
            h0 = jnp.zeros((bsz, SSM_STATE, ds), F32)
        tail0 = jnp.pad(prev, ((0, 0), (SUBLANES - (D_CONV - 1), 0), (0, 0)))
        ys, h_t = _ssd(xbc_in, dt_in, z_in, p["a_log"][i], p["d_skip"][i], p["ssm_norm"][i],
                       p["conv_w"][i], p["conv_b"][i], tail0, h0)
        ys = ys[:, :t].reshape(m, ds)
        convs.append(jnp.concatenate([prev, xbc3], axis=1)[:, t:])
        ssms.append(jnp.transpose(h_t, (0, 2, 1)).reshape(bsz, SSM_HEADS, SSM_HEAD_DIM, SSM_STATE))
        lams = tuple(p[n][i].reshape(1, ATT_HEAD_DIM).astype(F32)
                     for n in ("lambda_q1", "lambda_k1", "lambda_q2", "lambda_k2"))
        q3, k3, v3 = (a.reshape(bsz, t, dqk) for a in (q, k16, v16))
        if decode:
            o = _attn_decode(q3, k3, v3, cache_k, cache_v, i, lams, p["subln"][i], lam_init)
        else:
            o = _attn_prompt(q3, k3, v3, lams, p["subln"][i], lam_init)
        mix = _merge(ys, o.reshape(m, dqk), p["w_ssm_out"], p["w_att_out"], i, g, p["b_gate"][i])
        x, xn = _resid_norm(mix, p["w_o"], i, x, p["norm_ffn2"][i], 1.0)
        h = _swiglu_up(xn, p["w13_ffn2"], i)
        if i + 1 < depth:
            x, xn = _resid_norm(h, p["w2_ffn2"], i, x, p["norm_ffn1"][i + 1], 0.5)
        else:
            y = _resid_norm(h, p["w2_ffn2"], i, x, p["norm_final"], 0.5, final=True)
    if decode:
        k_all = kstack.reshape(depth, bsz, t, ATT_HEADS, 2, ATT_HEAD_DIM)
    else:
        k_all = jnp.transpose(kstack.reshape(depth, bsz, ATT_HEADS, 2, ATT_HEAD_DIM, t), (0, 1, 5, 2, 3, 4))
    v_all = vstack.reshape(depth, bsz, t, ATT_HEADS, 2 * ATT_HEAD_DIM)
    return y.reshape(bsz, t, d), k_all, v_all, jnp.stack(convs), jnp.stack(ssms)


def kernel(x_prompt, x_sample, cache_k, cache_v, state_conv, state_ssm, norm_ffn1, w13_ffn1, w2_ffn1, norm_mix, w_in, conv_w, conv_b, dt_bias, a_log, d_skip, ssm_norm, w_ssm_out, lambda_q1, lambda_k1, lambda_q2, lambda_k2, subln, w_att_out, b_gate, w_o, norm_ffn2, w13_ffn2, w2_ffn2, norm_final):
    d_model = x_prompt.shape[-1]
    w_in16, w_dt, w_q, w_k, w_v, w_g = _split_w_in(w_in, d_model)
    p = dict(
        norm_ffn1=norm_ffn1, w13_ffn1=w13_ffn1.astype(BF16), w2_ffn1=w2_ffn1.astype(BF16), norm_mix=norm_mix,
        w_in=w_in16, w_dt=w_dt, w_q=w_q, w_k=w_k, w_v=w_v, w_g=w_g,
        conv_w=conv_w, conv_b=conv_b, dt_bias=dt_bias, a_log=a_log, d_skip=d_skip, ssm_norm=ssm_norm,
        w_ssm_out=w_ssm_out.astype(BF16), lambda_q1=lambda_q1, lambda_k1=lambda_k1, lambda_q2=lambda_q2,
        lambda_k2=lambda_k2, subln=subln, w_att_out=w_att_out.astype(BF16), b_gate=b_gate,
        w_o=w_o.astype(BF16), norm_ffn2=norm_ffn2, w13_ffn2=w13_ffn2.astype(BF16), w2_ffn2=w2_ffn2.astype(BF16),
        norm_final=norm_final)
    depth, dec_b, n_past = cache_k.shape[:3]
    dqk = ATT_HEADS * 2 * ATT_HEAD_DIM
    y_p, k_p, v_p, conv_p, ssm_p = _trunk(x_prompt, jnp.arange(x_prompt.shape[1]), None, None, None, None, p)
    ck = jnp.transpose(cache_k, (0, 1, 3, 4, 5, 2)).reshape(depth, dec_b, dqk, n_past)
    cv = cache_v.reshape(depth, dec_b, n_past * ATT_HEADS, 2 * ATT_HEAD_DIM)
    y_s, k_s, v_s, conv_s, ssm_s = _trunk(x_sample, n_past + jnp.arange(x_sample.shape[1]), ck, cv,
                                          state_conv, state_ssm, p)
    return (y_p, y_s, k_p, v_p, conv_p, ssm_p, k_s, v_s, conv_s, ssm_s)
```
